```python
import math, functools
import jax, jax.numpy as jnp
from jax import lax
import numpy as np

D_MODEL = 1024
BATCH = 2
SEQ = 16384
DEPTH = 1
DEC_BATCH = 128
DEC_SEQ = 8
PAST_LEN = 8192
PAGE_SIZE = 128

ATT_HEADS = 8
KV_HEADS = 4
HEAD_DIM = 64
ATT_W = ATT_HEADS * HEAD_DIM
KV_W = KV_HEADS * HEAD_DIM
IDX_HEADS = 4
IDX_DIM = 64
TOPK_MAX = 256
Q_BLOCK = 128
ROPE_THETA = 10000.0
RWKV_HEADS = 8
RWKV_N = 64
RWKV_W = RWKV_HEADS * RWKV_N
LORA_W = 64
LORA_A = 64
LORA_G = 128
LN_X_EPS = 64e-5
SHIFT_W = 3 * RWKV_W + LORA_W + LORA_A + LORA_G
MEM_TOKENS = 256
MEM_HEADS = 4
MEM_HD = 128
MEM_W = MEM_HEADS * MEM_HD
PEER_HEADS = 8
PEER_DK = 128
N_KEYS = 128
N_EXPERTS = N_KEYS * N_KEYS
PEER_TOPK = 16
PEER_BLOCK = 128
NORM_EPS = 1e-6
SPLITS = (ATT_W, KV_W, KV_W, IDX_HEADS * IDX_DIM, IDX_DIM, IDX_HEADS, SHIFT_W, 2 * D_MODEL)
IN_COLS = ATT_W + 2 * KV_W + IDX_HEADS * IDX_DIM + IDX_DIM + IDX_HEADS + SHIFT_W + 2 * D_MODEL

kernel_name = 'dsa_rwkv7_peer_hybrid_step'


def rmsnorm(x, g):
    xf = x.astype(jnp.float32)
    y = xf * lax.rsqrt(jnp.mean(xf * xf, axis=-1, keepdims=True) + NORM_EPS)
    return (y * g.astype(jnp.float32)).astype(x.dtype)


def rope(x, pos):
    half = x.shape[-1] // 2
    inv = ROPE_THETA ** (-jnp.arange(half, dtype=jnp.float32) / half)
    ang = pos.astype(jnp.float32)[:, None] * inv[None, :]
    cos = jnp.cos(ang)[None, :, None, :]
    sin = jnp.sin(ang)[None, :, None, :]
    x1 = x[..., :half].astype(jnp.float32)
    x2 = x[..., half:].astype(jnp.float32)
    return jnp.concatenate([x1 * cos - x2 * sin, x2 * cos + x1 * sin], axis=-1).astype(x.dtype)


def indexer_scores(qi, wi, ki):
    dots = jax.nn.relu(jnp.einsum('bthd,bsd->bths', qi, ki).astype(jnp.float32))
    return jnp.einsum('bths,bth->bts', dots, wi.astype(jnp.float32)) * (IDX_DIM ** -0.5 * IDX_HEADS ** -0.5)


def sparse_gqa(q, k_sel, v_sel, valid):
    B, T = q.shape[:2]
    qg = q.reshape(B, T, KV_HEADS, ATT_HEADS // KV_HEADS, HEAD_DIM)
    logits = jnp.einsum('btngd,btsnd->btngs', qg, k_sel).astype(jnp.float32) * (HEAD_DIM ** -0.5)
    logits = jnp.where(valid[:, :, None, None, :], logits, -jnp.inf)
    p = jax.nn.softmax(logits, axis=-1).astype(v_sel.dtype)
    return jnp.einsum('btngs,btsnd->btngd', p, v_sel).reshape(B, T, ATT_HEADS, HEAD_DIM)


def dsa_prompt(q, k, v, qi, ki, wi):
    B, S = q.shape[:2]
    k_top = min(TOPK_MAX, S // 4)
    nblk = S // Q_BLOCK
    blk = lambda t: jnp.moveaxis(t.reshape((B, nblk, Q_BLOCK) + t.shape[2:]), 1, 0)
    b_ix = jnp.arange(B)[:, None, None]
    key_pos = jnp.arange(S)

    def one(args):
        j, qb, qib, wib = args
        t = j * Q_BLOCK + jnp.arange(Q_BLOCK)
        score = indexer_scores(qib, wib, ki)
        score = jnp.where((key_pos[None, :] <= t[:, None])[None], score, -jnp.inf)
        _, sel = lax.top_k(score, k_top)
        valid = sel <= t[None, :, None]
        return sparse_gqa(qb, k[b_ix, sel], v[b_ix, sel], valid)

    out = lax.map(one, (jnp.arange(nblk), blk(q), blk(qi), blk(wi)))
    return jnp.moveaxis(out, 0, 1).reshape(B, S, ATT_W)


def dsa_sample(q, k, v, qi, ki, wi, cache_k, cache_v, cache_idx_k, page_table):
    Bd, T = q.shape[:2]
    n_pages = page_table.shape[1]
    past = n_pages * PAGE_SIZE
    L = past + T
    k_top = min(TOPK_MAX, L // 4)
    ki_past = cache_idx_k[page_table].reshape(Bd, past, IDX_DIM)
    ki_all = jnp.concatenate([ki_past, ki.astype(ki_past.dtype)], axis=1)
    t = past + jnp.arange(T)
    score = indexer_scores(qi, wi, ki_all)
    score = jnp.where((jnp.arange(L)[None, :] <= t[:, None])[None], score, -jnp.inf)
    _, sel = lax.top_k(score, k_top)
    valid = sel <= t[None, :, None]
    in_past = (sel < past)[..., None, None]
    ps = jnp.clip(sel, 0, past - 1)
    phys = jnp.take_along_axis(page_table, (ps // PAGE_SIZE).reshape(Bd, -1), axis=1).reshape(ps.shape)
    row = phys * PAGE_SIZE + ps % PAGE_SIZE
    flat_k = cache_k.reshape(-1, KV_HEADS, HEAD_DIM)
    flat_v = cache_v.reshape(-1, KV_HEADS, HEAD_DIM)
    ns = jnp.clip(sel - past, 0, T - 1)
    b_ix = jnp.arange(Bd)[:, None, None]
    k_sel = jnp.where(in_past, flat_k[row], k[b_ix, ns].astype(flat_k.dtype))
    v_sel = jnp.where(in_past, flat_v[row], v[b_ix, ns].astype(flat_v.dtype))
    return sparse_gqa(q, k_sel, v_sel, valid).reshape(Bd, T, ATT_W)


def wkv_scan(r, w, k, v, kk, a, s0):
    def step(S, inp):
        r_t, w_t, k_t, v_t, kk_t, a_t = inp
        sa = jnp.einsum('bhij,bhj->bhi', S, -kk_t)
        S = S * w_t[:, :, None, :] + sa[..., None] * (kk_t * a_t)[:, :, None, :] + v_t[..., None] * k_t[:, :, None, :]
        return S, jnp.einsum('bhij,bhj->bhi', S, r_t)
    xs = tuple(jnp.moveaxis(t.astype(jnp.float32), 1, 0) for t in (r, w, k, v, kk, a))
    s_fin, ys = lax.scan(step, s0.astype(jnp.float32), xs)
    return jnp.moveaxis(ys, 0, 1), s_fin


def rwkv_branch(feat, prev, s0, lw):
    B, T, _ = feat.shape
    f32 = jnp.float32
    shifted = jnp.concatenate([prev[:, None, :].astype(feat.dtype), feat[:, :-1]], axis=1)
    mixed = feat + (shifted - feat) * lw['rwkv_mu']
    r, k, v, wd, ad, gd = jnp.split(mixed, [RWKV_W, 2 * RWKV_W, 3 * RWKV_W, 3 * RWKV_W + LORA_W, 3 * RWKV_W + LORA_W + LORA_A], axis=-1)
    log_w = -jax.nn.softplus(-(lw['rwkv_w0'] + jnp.tanh(wd) @ lw['rwkv_w2']).astype(f32)) - 0.5
    decay = jnp.exp(-jnp.exp(log_w))
    a = jax.nn.sigmoid((lw['rwkv_a0'] + ad @ lw['rwkv_a2']).astype(f32))
    g = jax.nn.sigmoid(gd.astype(f32)) @ lw['rwkv_g2'].astype(f32)
    heads = lambda t: t.reshape(B, T, RWKV_HEADS, RWKV_N)
    kk = heads(k.astype(f32) * lw['rwkv_k_k'].astype(f32))
    kk = kk / jnp.maximum(jnp.sqrt(jnp.sum(kk * kk, axis=-1, keepdims=True)), 1e-12)
    k_eff = k.astype(f32) * (1.0 + (a - 1.0) * lw['rwkv_k_a'].astype(f32))
    rh, kh, vh, ah = heads(r.astype(f32)), heads(k_eff), heads(v.astype(f32)), heads(a)
    y, s_fin = wkv_scan(rh, heads(decay), kh, vh, kk, ah, s0)
    mean = jnp.mean(y, axis=-1, keepdims=True)
    var = jnp.mean(jnp.square(y - mean), axis=-1, keepdims=True)
    yn = ((y - mean) * lax.rsqrt(var + LN_X_EPS)).reshape(B, T, RWKV_W) * lw['rwkv_ln_g'].astype(f32) + lw['rwkv_ln_b'].astype(f32)
    bonus = jnp.sum(rh * kh * lw['rwkv_r_k'].astype(f32), axis=-1, keepdims=True) * vh
    out = (yn + bonus.reshape(B, T, RWKV_W)) * g
    return out.astype(feat.dtype), feat[:, -1], s_fin


def mem_kv(mem, lw):
    B, M, _ = mem.shape
    mn = rmsnorm(mem, lw['norm_mem'])
    return (mn @ lw['w_mk']).reshape(B, M, MEM_HEADS, MEM_HD), (mn @ lw['w_mv']).reshape(B, M, MEM_HEADS, MEM_HD)


def cross_attn(x, mk, mv, lw):
    B, T, _ = x.shape
    q = (rmsnorm(x, lw['norm_ca']) @ lw['w_cq']).reshape(B, T, MEM_HEADS, MEM_HD)
    logits = jnp.einsum('bthd,bmhd->bhtm', q, mk.astype(q.dtype)).astype(jnp.float32) * (MEM_HD ** -0.5)
    p = jax.nn.softmax(logits, axis=-1).astype(x.dtype)
    o = jnp.einsum('bhtm,bmhd->bthd', p, mv.astype(x.dtype)).reshape(B, T, MEM_W)
    return x + o @ lw['w_co']


def peer(xn, lw):
    B, T, D = xn.shape
    n = B * T
    pad = (-n) % PEER_BLOCK
    xb = jnp.pad(xn.reshape(n, D), ((0, pad), (0, 0))).reshape(-1, PEER_BLOCK, D)
    wq, k1, k2, u_tab, v_tab = lw['peer_wq'], lw['peer_k1'], lw['peer_k2'], lw['peer_u'], lw['peer_v']

    def one(xt):
        q = (xt @ wq).reshape(PEER_BLOCK, PEER_HEADS, 2, PEER_DK // 2)
        s1 = jnp.einsum('nhd,hkd->nhk', q[:, :, 0], k1).astype(jnp.float32)
        s2 = jnp.einsum('nhd,hkd->nhk', q[:, :, 1], k2).astype(jnp.float32)
        v1, i1 = lax.top_k(s1, PEER_TOPK)
        v2, i2 = lax.top_k(s2, PEER_TOPK)
        cand_s = (v1[..., :, None] + v2[..., None, :]).reshape(PEER_BLOCK, PEER_HEADS, PEER_TOPK * PEER_TOPK)
        cand_e = (i1[..., :, None] * N_KEYS + i2[..., None, :]).reshape(PEER_BLOCK, PEER_HEADS, PEER_TOPK * PEER_TOPK)
        top_s, pos = lax.top_k(cand_s, PEER_TOPK)
        e = jnp.take_along_axis(cand_e, pos, axis=-1)
        g = jax.nn.softmax(top_s, axis=-1)
        act = jax.nn.gelu(jnp.einsum('nhkd,nd->nhk', u_tab[e], xt).astype(jnp.float32), approximate=False)
        return jnp.einsum('nhk,nhkd->nd', (g * act).astype(xt.dtype), v_tab[e])

    out = lax.map(one, xb)
    return out.reshape(-1, D)[:n].reshape(B, T, D)


def trunk_layer(x, pos, lw, attend, shift_prev, wkv0, mk, mv):
    B, T, _ = x.shape
    h = rmsnorm(x, lw['norm_mix'])
    points = np.cumsum(SPLITS)[:-1].tolist()
    q, k, v, qi, ki, wi, feat, gates = jnp.split(h @ lw['w_in'], points, axis=-1)
    q = rope(q.reshape(B, T, ATT_HEADS, HEAD_DIM), pos)
    k = rope(k.reshape(B, T, KV_HEADS, HEAD_DIM), pos)
    v = v.reshape(B, T, KV_HEADS, HEAD_DIM)
    qi = rope(qi.reshape(B, T, IDX_HEADS, IDX_DIM), pos)
    ki = rope(ki.reshape(B, T, 1, IDX_DIM), pos)[:, :, 0]
    o_a = attend(q, k, v, qi, ki, wi)
    o_b, shift_last, wkv_fin = rwkv_branch(feat, shift_prev, wkv0, lw)
    g_a, g_b = jnp.split(jax.nn.sigmoid(gates.astype(jnp.float32)).astype(x.dtype), 2, axis=-1)
    x = x + (g_a * (o_a @ lw['proj_a']) + g_b * (o_b @ lw['proj_b'])) @ lw['w_out']
    x = cross_attn(x, mk, mv, lw)
    x = x + peer(rmsnorm(x, lw['norm_ffn']), lw)
    return x, (k, v, ki, shift_last, wkv_fin)


def setup_inputs(seed: int = 0) -> dict:
    key = jax.random.key(seed)
    ks = iter(jax.random.split(key, 64))
    nrm = lambda shape, scale: scale * jax.random.normal(next(ks), shape, jnp.float32)
    gain = lambda shape: 1.0 + nrm(shape, 0.02)
    L = DEPTH
    n_pages = PAST_LEN // PAGE_SIZE
    n_used = DEC_BATCH * n_pages
    n_phys = n_used + max(1, n_used // 4)
    page_table = jax.random.permutation(next(ks), n_phys)[:n_used].reshape(DEC_BATCH, n_pages).astype(jnp.int32)
    return {
        'x_prompt': nrm((BATCH, SEQ, D_MODEL), 1.0),
        'x_sample': nrm((DEC_BATCH, DEC_SEQ, D_MODEL), 1.0),
        'cache_k': nrm((L, n_phys, PAGE_SIZE, KV_HEADS, HEAD_DIM), 1.0),
        'cache_v': nrm((L, n_phys, PAGE_SIZE, KV_HEADS, HEAD_DIM), 1.0),
        'cache_idx_k': nrm((L, n_phys, PAGE_SIZE, IDX_DIM), 1.0),
        'state_shift': nrm((L, DEC_BATCH, SHIFT_W), 1.0),
        'state_wkv': nrm((L, DEC_BATCH, RWKV_HEADS, RWKV_N, RWKV_N), 0.3),
        'cache_mem_k': nrm((L, DEC_BATCH, MEM_TOKENS, MEM_HEADS, MEM_HD), 1.0),
        'cache_mem_v': nrm((L, DEC_BATCH, MEM_TOKENS, MEM_HEADS, MEM_HD), 1.0),
        'page_table': page_table,
        'mem_prompt': nrm((BATCH, MEM_TOKENS, D_MODEL), 1.0),
        'norm_mix': gain((L, D_MODEL)),
        'w_in': nrm((L, D_MODEL, IN_COLS), D_MODEL ** -0.5),
        'rwkv_mu': jax.random.uniform(next(ks), (L, SHIFT_W), jnp.float32),
        'rwkv_w0': nrm((L, RWKV_W), 0.5),
        'rwkv_w2': nrm((L, LORA_W, RWKV_W), LORA_W ** -0.5),
        'rwkv_a0': nrm((L, RWKV_W), 0.5),
        'rwkv_a2': nrm((L, LORA_A, RWKV_W), LORA_A ** -0.5),
        'rwkv_g2': nrm((L, LORA_G, RWKV_W), LORA_G ** -0.5),
        'rwkv_k_k': 0.85 + nrm((L, RWKV_W), 0.05),
        'rwkv_k_a': 1.0 + nrm((L, RWKV_W), 0.05),
        'rwkv_r_k': nrm((L, RWKV_HEADS, RWKV_N), 0.1),
        'rwkv_ln_g': gain((L, RWKV_W)),
        'rwkv_ln_b': nrm((L, RWKV_W), 0.02),
        'proj_a': nrm((L, ATT_W, D_MODEL), ATT_W ** -0.5),
        'proj_b': nrm((L, RWKV_W, D_MODEL), RWKV_W ** -0.5),
        'w_out': nrm((L, D_MODEL, D_MODEL), D_MODEL ** -0.5),
        'norm_ca': gain((L, D_MODEL)),
        'norm_mem': gain((L, D_MODEL)),
        'w_cq': nrm((L, D_MODEL, MEM_W), D_MODEL ** -0.5),
        'w_mk': nrm((L, D_MODEL, MEM_W), D_MODEL ** -0.5),
        'w_mv': nrm((L, D_MODEL, MEM_W), D_MODEL ** -0.5),
        'w_co': nrm((L, MEM_W, D_MODEL), MEM_W ** -0.5),
        'norm_ffn': gain((L, D_MODEL)),
        'peer_wq': nrm((L, D_MODEL, PEER_HEADS * PEER_DK), D_MODEL ** -0.5),
        'peer_k1': nrm((L, PEER_HEADS, N_KEYS, PEER_DK // 2), (PEER_DK // 2) ** -0.5),
        'peer_k2': nrm((L, PEER_HEADS, N_KEYS, PEER_DK // 2), (PEER_DK // 2) ** -0.5),
        'peer_u': nrm((L, N_EXPERTS, D_MODEL), D_MODEL ** -0.5),
        'peer_v': nrm((L, N_EXPERTS, D_MODEL), 0.3),
        'norm_final': gain((D_MODEL,)),
    }


def reference(x_prompt, x_sample, cache_k, cache_v, cache_idx_k, state_shift, state_wkv, cache_mem_k, cache_mem_v, page_table, mem_prompt, norm_mix, w_in, rwkv_mu, rwkv_w0, rwkv_w2, rwkv_a0, rwkv_a2, rwkv_g2, rwkv_k_k, rwkv_k_a, rwkv_r_k, rwkv_ln_g, rwkv_ln_b, proj_a, proj_b, w_out, norm_ca, norm_mem, w_cq, w_mk, w_mv, w_co, norm_ffn, peer_wq, peer_k1, peer_k2, peer_u, peer_v, norm_final):
    B, S, _ = x_prompt.shape
    Bd, T, _ = x_sample.shape
    past = page_table.shape[1] * PAGE_SIZE
    pos_p = jnp.arange(S)
    pos_s = past + jnp.arange(T)
    xp, xs = x_prompt, x_sample
    st_p_all, st_s_all, mem_all = [], [], []
    for l in range(DEPTH):
        lw = {
            'norm_mix': norm_mix[l], 'w_in': w_in[l], 'rwkv_mu': rwkv_mu[l], 'rwkv_w0': rwkv_w0[l],
            'rwkv_w2': rwkv_w2[l], 'rwkv_a0': rwkv_a0[l], 'rwkv_a2': rwkv_a2[l], 'rwkv_g2': rwkv_g2[l],
            'rwkv_k_k': rwkv_k_k[l], 'rwkv_k_a': rwkv_k_a[l], 'rwkv_r_k': rwkv_r_k[l], 'rwkv_ln_g': rwkv_ln_g[l],
            'rwkv_ln_b': rwkv_ln_b[l], 'proj_a': proj_a[l], 'proj_b': proj_b[l], 'w_out': w_out[l],
            'norm_ca': norm_ca[l], 'norm_mem': norm_mem[l], 'w_cq': w_cq[l], 'w_mk': w_mk[l], 'w_mv': w_mv[l],
            'w_co': w_co[l], 'norm_ffn': norm_ffn[l], 'peer_wq': peer_wq[l], 'peer_k1': peer_k1[l],
            'peer_k2': peer_k2[l], 'peer_u': peer_u[l], 'peer_v': peer_v[l],
        }
        mk_p, mv_p = mem_kv(mem_prompt, lw)
        xp, st_p = trunk_layer(xp, pos_p, lw, dsa_prompt,
                               jnp.zeros((B, SHIFT_W), x_prompt.dtype),
                               jnp.zeros((B, RWKV_HEADS, RWKV_N, RWKV_N), jnp.float32), mk_p, mv_p)
        attend_s = functools.partial(dsa_sample, cache_k=cache_k[l], cache_v=cache_v[l],
                                     cache_idx_k=cache_idx_k[l], page_table=page_table)
        xs, st_s = trunk_layer(xs, pos_s, lw, attend_s, state_shift[l], state_wkv[l], cache_mem_k[l], cache_mem_v[l])
        st_p_all.append(st_p)
        st_s_all.append(st_s)
        mem_all.append((mk_p, mv_p))
    y_prompt = rmsnorm(xp, norm_final)
    y_sample = rmsnorm(xs, norm_final)
    new_k_prompt = jnp.stack([s[0] for s in st_p_all])
    new_v_prompt = jnp.stack([s[1] for s in st_p_all])
    new_idx_k_prompt = jnp.stack([s[2] for s in st_p_all])
    new_shift_prompt = jnp.stack([s[3] for s in st_p_all])
    new_wkv_prompt = jnp.stack([s[4] for s in st_p_all])
    new_mem_k_prompt = jnp.stack([m[0] for m in mem_all])
    new_mem_v_prompt = jnp.stack([m[1] for m in mem_all])
    new_k_sample = jnp.stack([s[0] for s in st_s_all])
    new_v_sample = jnp.stack([s[1] for s in st_s_all])
    new_idx_k_sample = jnp.stack([s[2] for s in st_s_all])
    new_shift_sample = jnp.stack([s[3] for s in st_s_all])
    new_wkv_sample = jnp.stack([s[4] for s in st_s_all])
    return (y_prompt, y_sample, new_k_prompt, new_v_prompt, new_idx_k_prompt, new_shift_prompt, new_wkv_prompt, new_mem_k_prompt, new_mem_v_prompt, new_k_sample, new_v_sample, new_idx_k_sample, new_shift_sample, new_wkv_sample)
```

```python
import functools
import math

import jax
import jax.numpy as jnp
import numpy as np
from jax import lax
from jax.experimental import pallas as pl
from jax.experimental.pallas import tpu as pltpu

D_MODEL = 1024
PAGE_SIZE = 128
ATT_HEADS = 8
KV_HEADS = 4
HEAD_DIM = 64
ATT_W = ATT_HEADS * HEAD_DIM
KV_W = KV_HEADS * HEAD_DIM
IDX_HEADS = 4
IDX_DIM = 64
TOPK_MAX = 256
Q_BLOCK = 128
ROPE_THETA = 10000.0
RWKV_HEADS = 8
RWKV_N = 64
RWKV_W = RWKV_HEADS * RWKV_N
LORA_W = 64
LORA_A = 64
LORA_G = 128
LN_X_EPS = 64e-5
SHIFT_W = 3 * RWKV_W + LORA_W + LORA_A + LORA_G
MEM_HEADS = 4
MEM_HD = 128
MEM_W = MEM_HEADS * MEM_HD
PEER_HEADS = 8
PEER_DK = 128
N_KEYS = 128
PEER_TOPK = 16
PEER_BLOCK = 128
NORM_EPS = 1e-6
SPLITS = (ATT_W, KV_W, KV_W, IDX_HEADS * IDX_DIM, IDX_DIM, IDX_HEADS, SHIFT_W, 2 * D_MODEL)

LANES = 128
VMEM_LIMIT = 56 * 1024 * 1024

F32 = jnp.float32
BF16 = jnp.bfloat16


def _cparams(*sem):
    return pltpu.CompilerParams(dimension_semantics=sem, vmem_limit_bytes=VMEM_LIMIT)


def _rms(x, g):
    return x * lax.rsqrt(jnp.mean(x * x, axis=-1, keepdims=True) + NORM_EPS) * g


_SEG_Q = (0, 512)
_SEG_K = (512, 768)
_SEG_V = (768, 1024)
_SEG_QI = (1024, 1280)
_SEG_KIWI = (1280, 1408)
_SEG_FEAT = (1408, 3200)
_SEG_GATE = (3200, 5248)
_PACKED_COLS = 5248


def _pack_w_in(w_in):
    pts = np.cumsum(SPLITS)[:-1].tolist()
    wq, wk, wv, wqi, wki, wwi, wfeat, wgate = jnp.split(w_in, pts, axis=-1)
    pad = jnp.zeros((w_in.shape[0], LANES - IDX_DIM - IDX_HEADS), w_in.dtype)
    return jnp.concatenate([wq, wk, wv, wqi, wki, wwi, pad, wfeat, wgate], axis=-1).astype(BF16)


def _rope_lanes(y, cos, sin_signed, first_half):
    outs = []
    for c in range(y.shape[1] // LANES):
        xc = y[:, c * LANES:(c + 1) * LANES]
        partner = jnp.where(first_half, pltpu.roll(xc, 96, 1), pltpu.roll(xc, 32, 1))
        outs.append(xc * cos + partner * sin_signed)
    return outs[0] if len(outs) == 1 else jnp.concatenate(outs, axis=1)


def _in_proj_kernel(x_ref, g_ref, w_ref, inv_ref, q_ref, k_ref, v_ref, qi_ref, kiwi_ref, feat_ref, ga_ref, gb_ref,
                    *, tm, period, offset):
    x = x_ref[...]
    h = _rms(x, g_ref[...]).astype(BF16)
    row = pl.program_id(0) * tm + lax.broadcasted_iota(jnp.int32, (tm, LANES), 0)
    pos = (row % period + offset).astype(F32)
    ang = pos * inv_ref[...]
    cos = jnp.cos(ang)
    sin = jnp.sin(ang)
    lane = lax.broadcasted_iota(jnp.int32, (tm, LANES), 1)
    first_half = (lane % HEAD_DIM) < (HEAD_DIM // 2)
    sin_signed = jnp.where(first_half, -sin, sin)

    def seg(s):
        return jnp.dot(h, w_ref[:, s[0]:s[1]], preferred_element_type=F32)

    q_ref[...] = _rope_lanes(seg(_SEG_Q), cos, sin_signed, first_half)
    k_ref[...] = _rope_lanes(seg(_SEG_K), cos, sin_signed, first_half)
    v_ref[...] = seg(_SEG_V)
    qi_ref[...] = _rope_lanes(seg(_SEG_QI), cos, sin_signed, first_half)
    kiwi = seg(_SEG_KIWI)
    kiwi_ref[...] = jnp.where(lane < IDX_DIM, _rope_lanes(kiwi, cos, sin_signed, first_half), kiwi)
    feat_ref[...] = seg(_SEG_FEAT)
    gates = jax.nn.sigmoid(seg(_SEG_GATE))
    ga_ref[...] = gates[:, :D_MODEL]
    gb_ref[...] = gates[:, D_MODEL:]


def in_proj(x2d, norm_g, w_packed, inv_tiled, *, period, offset, tm=256):
    n = x2d.shape[0]
    assert n % tm == 0
    widths = (ATT_W, KV_W, KV_W, IDX_HEADS * IDX_DIM, LANES, SHIFT_W, D_MODEL, D_MODEL)
    row = lambda w: pl.BlockSpec((tm, w), lambda i: (i, 0))
    full = lambda a: pl.BlockSpec(a.shape, lambda i: (0,) * a.ndim)
    return pl.pallas_call(
        functools.partial(_in_proj_kernel, tm=tm, period=period, offset=offset),
        grid=(n // tm,),
        in_specs=[row(D_MODEL), full(norm_g), full(w_packed), full(inv_tiled)],
        out_specs=[row(w) for w in widths],
        out_shape=[jax.ShapeDtypeStruct((n, w), F32) for w in widths],
        compiler_params=_cparams("parallel"),
        name="in_proj",
    )(x2d, norm_g, w_packed, inv_tiled)


def _mix_out_kernel(x_ref, oa_ref, ob_ref, ga_ref, gb_ref, pa_ref, pb_ref, wo_ref, o_ref):
    ya = jnp.dot(oa_ref[...].astype(BF16), pa_ref[...], preferred_element_type=F32)
    yb = jnp.dot(ob_ref[...].astype(BF16), pb_ref[...], preferred_element_type=F32)
    m = ga_ref[...] * ya + gb_ref[...] * yb
    o_ref[...] = x_ref[...] + jnp.dot(m.astype(BF16), wo_ref[...], preferred_element_type=F32)


def mix_out(x2d, oa, ob, ga, gb, pa, pb, wo, *, tm=256):
    n = x2d.shape[0]
    row = lambda w: pl.BlockSpec((tm, w), lambda i: (i, 0))
    full = lambda a: pl.BlockSpec(a.shape, lambda i: (0,) * a.ndim)
    return pl.pallas_call(
        _mix_out_kernel,
        grid=(n // tm,),
        in_specs=[row(D_MODEL), row(ATT_W), row(RWKV_W), row(D_MODEL), row(D_MODEL), full(pa), full(pb), full(wo)],
        out_specs=row(D_MODEL),
        out_shape=jax.ShapeDtypeStruct((n, D_MODEL), F32),
        compiler_params=_cparams("parallel"),
        name="mix_out",
    )(x2d, oa, ob, ga, gb, pa, pb, wo)


def rmsnorm(x, g):
    xf = x.astype(jnp.float32)
    y = xf * lax.rsqrt(jnp.mean(xf * xf, axis=-1, keepdims=True) + NORM_EPS)
    return (y * g.astype(jnp.float32)).astype(x.dtype)


def indexer_scores(qi, wi, ki):
    dots = jax.nn.relu(jnp.einsum('bthd,bsd->bths', qi, ki).astype(jnp.float32))
    return jnp.einsum('bths,bth->bts', dots, wi.astype(jnp.float32)) * (IDX_DIM ** -0.5 * IDX_HEADS ** -0.5)


def sparse_gqa(q, k_sel, v_sel, valid):
    B, T = q.shape[:2]
    qg = q.reshape(B, T, KV_HEADS, ATT_HEADS // KV_HEADS, HEAD_DIM)
    logits = jnp.einsum('btngd,btsnd->btngs', qg, k_sel).astype(jnp.float32) * (HEAD_DIM ** -0.5)
    logits = jnp.where(valid[:, :, None, None, :], logits, -jnp.inf)
    p = jax.nn.softmax(logits, axis=-1).astype(v_sel.dtype)
    return jnp.einsum('btngs,btsnd->btngd', p, v_sel).reshape(B, T, ATT_HEADS, HEAD_DIM)


def dsa_prompt(q, k, v, qi, ki, wi):
    B, S = q.shape[:2]
    k_top = min(TOPK_MAX, S // 4)
    nblk = S // Q_BLOCK
    blk = lambda t: jnp.moveaxis(t.reshape((B, nblk, Q_BLOCK) + t.shape[2:]), 1, 0)
    b_ix = jnp.arange(B)[:, None, None]
    key_pos = jnp.arange(S)

    def one(args):
        j, qb, qib, wib = args
        t = j * Q_BLOCK + jnp.arange(Q_BLOCK)
        score = indexer_scores(qib, wib, ki)
        score = jnp.where((key_pos[None, :] <= t[:, None])[None], score, -jnp.inf)
        _, sel = lax.top_k(score, k_top)
        valid = sel <= t[None, :, None]
        return sparse_gqa(qb, k[b_ix, sel], v[b_ix, sel], valid)

    out = lax.map(one, (jnp.arange(nblk), blk(q), blk(qi), blk(wi)))
    return jnp.moveaxis(out, 0, 1).reshape(B, S, ATT_W)


def dsa_sample(q, k, v, qi, ki, wi, cache_k, cache_v, cache_idx_k, page_table):
    Bd, T = q.shape[:2]
    n_pages = page_table.shape[1]
    past = n_pages * PAGE_SIZE
    L = past + T
    k_top = min(TOPK_MAX, L // 4)
    ki_past = cache_idx_k[page_table].reshape(Bd, past, IDX_DIM)
    ki_all = jnp.concatenate([ki_past, ki.astype(ki_past.dtype)], axis=1)
    t = past + jnp.arange(T)
    score = indexer_scores(qi, wi, ki_all)
    score = jnp.where((jnp.arange(L)[None, :] <= t[:, None])[None], score, -jnp.inf)
    _, sel = lax.top_k(score, k_top)
    valid = sel <= t[None, :, None]
    in_past = (sel < past)[..., None, None]
    ps = jnp.clip(sel, 0, past - 1)
    phys = jnp.take_along_axis(page_table, (ps // PAGE_SIZE).reshape(Bd, -1), axis=1).reshape(ps.shape)
    row = phys * PAGE_SIZE + ps % PAGE_SIZE
    flat_k = cache_k.reshape(-1, KV_HEADS, HEAD_DIM)
    flat_v = cache_v.reshape(-1, KV_HEADS, HEAD_DIM)
    ns = jnp.clip(sel - past, 0, T - 1)
    b_ix = jnp.arange(Bd)[:, None, None]
    k_sel = jnp.where(in_past, flat_k[row], k[b_ix, ns].astype(flat_k.dtype))
    v_sel = jnp.where(in_past, flat_v[row], v[b_ix, ns].astype(flat_v.dtype))
    return sparse_gqa(q, k_sel, v_sel, valid).reshape(Bd, T, ATT_W)


def wkv_scan(r, w, k, v, kk, a, s0):
    def step(S, inp):
        r_t, w_t, k_t, v_t, kk_t, a_t = inp
        sa = jnp.einsum('bhij,bhj->bhi', S, -kk_t)
        S = S * w_t[:, :, None, :] + sa[..., None] * (kk_t * a_t)[:, :, None, :] + v_t[..., None] * k_t[:, :, None, :]
        return S, jnp.einsum('bhij,bhj->bhi', S, r_t)
    xs = tuple(jnp.moveaxis(t.astype(jnp.float32), 1, 0) for t in (r, w, k, v, kk, a))
    s_fin, ys = lax.scan(step, s0.astype(jnp.float32), xs)
    return jnp.moveaxis(ys, 0, 1), s_fin


def rwkv_branch(feat, prev, s0, lw):
    B, T, _ = feat.shape
    f32 = jnp.float32
    shifted = jnp.concatenate([prev[:, None, :].astype(feat.dtype), feat[:, :-1]], axis=1)
    mixed = feat + (shifted - feat) * lw['rwkv_mu']
    r, k, v, wd, ad, gd = jnp.split(mixed, [RWKV_W, 2 * RWKV_W, 3 * RWKV_W, 3 * RWKV_W + LORA_W, 3 * RWKV_W + LORA_W + LORA_A], axis=-1)
    log_w = -jax.nn.softplus(-(lw['rwkv_w0'] + jnp.tanh(wd) @ lw['rwkv_w2']).astype(f32)) - 0.5
    decay = jnp.exp(-jnp.exp(log_w))
    a = jax.nn.sigmoid((lw['rwkv_a0'] + ad @ lw['rwkv_a2']).astype(f32))
    g = jax.nn.sigmoid(gd.astype(f32)) @ lw['rwkv_g2'].astype(f32)
    heads = lambda t: t.reshape(B, T, RWKV_HEADS, RWKV_N)
    kk = heads(k.astype(f32) * lw['rwkv_k_k'].astype(f32))
    kk = kk / jnp.maximum(jnp.sqrt(jnp.sum(kk * kk, axis=-1, keepdims=True)), 1e-12)
    k_eff = k.astype(f32) * (1.0 + (a - 1.0) * lw['rwkv_k_a'].astype(f32))
    rh, kh, vh, ah = heads(r.astype(f32)), heads(k_eff), heads(v.astype(f32)), heads(a)
    y, s_fin = wkv_scan(rh, heads(decay), kh, vh, kk, ah, s0)
    mean = jnp.mean(y, axis=-1, keepdims=True)
    var = jnp.mean(jnp.square(y - mean), axis=-1, keepdims=True)
    yn = ((y - mean) * lax.rsqrt(var + LN_X_EPS)).reshape(B, T, RWKV_W) * lw['rwkv_ln_g'].astype(f32) + lw['rwkv_ln_b'].astype(f32)
    bonus = jnp.sum(rh * kh * lw['rwkv_r_k'].astype(f32), axis=-1, keepdims=True) * vh
    out = (yn + bonus.reshape(B, T, RWKV_W)) * g
    return out.astype(feat.dtype), feat[:, -1], s_fin


def mem_kv(mem, lw):
    B, M, _ = mem.shape
    mn = rmsnorm(mem, lw['norm_mem'])
    return (mn @ lw['w_mk']).reshape(B, M, MEM_HEADS, MEM_HD), (mn @ lw['w_mv']).reshape(B, M, MEM_HEADS, MEM_HD)


def cross_attn(x, mk, mv, lw):
    B, T, _ = x.shape
    q = (rmsnorm(x, lw['norm_ca']) @ lw['w_cq']).reshape(B, T, MEM_HEADS, MEM_HD)
    logits = jnp.einsum('bthd,bmhd->bhtm', q, mk.astype(q.dtype)).astype(jnp.float32) * (MEM_HD ** -0.5)
    p = jax.nn.softmax(logits, axis=-1).astype(x.dtype)
    o = jnp.einsum('bhtm,bmhd->bthd', p, mv.astype(x.dtype)).reshape(B, T, MEM_W)
    return x + o @ lw['w_co']


def peer(xn, lw):
    B, T, D = xn.shape
    n = B * T
    pad = (-n) % PEER_BLOCK
    xb = jnp.pad(xn.reshape(n, D), ((0, pad), (0, 0))).reshape(-1, PEER_BLOCK, D)
    wq, k1, k2, u_tab, v_tab = lw['peer_wq'], lw['peer_k1'], lw['peer_k2'], lw['peer_u'], lw['peer_v']

    def one(xt):
        q = (xt @ wq).reshape(PEER_BLOCK, PEER_HEADS, 2, PEER_DK // 2)
        s1 = jnp.einsum('nhd,hkd->nhk', q[:, :, 0], k1).astype(jnp.float32)
        s2 = jnp.einsum('nhd,hkd->nhk', q[:, :, 1], k2).astype(jnp.float32)
        v1, i1 = lax.top_k(s1, PEER_TOPK)
        v2, i2 = lax.top_k(s2, PEER_TOPK)
        cand_s = (v1[..., :, None] + v2[..., None, :]).reshape(PEER_BLOCK, PEER_HEADS, PEER_TOPK * PEER_TOPK)
        cand_e = (i1[..., :, None] * N_KEYS + i2[..., None, :]).reshape(PEER_BLOCK, PEER_HEADS, PEER_TOPK * PEER_TOPK)
        top_s, pos = lax.top_k(cand_s, PEER_TOPK)
        e = jnp.take_along_axis(cand_e, pos, axis=-1)
        g = jax.nn.softmax(top_s, axis=-1)
        act = jax.nn.gelu(jnp.einsum('nhkd,nd->nhk', u_tab[e], xt).astype(jnp.float32), approximate=False)
        return jnp.einsum('nhk,nhkd->nd', (g * act).astype(xt.dtype), v_tab[e])

    out = lax.map(one, xb)
    return out.reshape(-1, D)[:n].reshape(B, T, D)


def _rope_inv_tiled():
    half = HEAD_DIM // 2
    inv = ROPE_THETA ** (-jnp.arange(half, dtype=jnp.float32) / half)
    return jnp.tile(inv, LANES // half)[None, :]


def trunk_layer(x, period, offset, lw, attend, shift_prev, wkv0, mk, mv):
    B, T, _ = x.shape
    n = B * T
    x2d = x.reshape(n, D_MODEL)
    q, k, v, qi, kiwi, feat, ga, gb = in_proj(x2d, lw['norm_mix'][None, :], lw['w_in_packed'], _rope_inv_tiled(),
                                              period=period, offset=offset)
    q = q.reshape(B, T, ATT_HEADS, HEAD_DIM)
    k = k.reshape(B, T, KV_HEADS, HEAD_DIM)
    v = v.reshape(B, T, KV_HEADS, HEAD_DIM)
    qi = qi.reshape(B, T, IDX_HEADS, IDX_DIM)
    ki = kiwi[:, :IDX_DIM].reshape(B, T, IDX_DIM)
    wi = kiwi[:, IDX_DIM:IDX_DIM + IDX_HEADS].reshape(B, T, IDX_HEADS)
    feat = feat.reshape(B, T, SHIFT_W)
    o_a = attend(q, k, v, qi, ki, wi)
    o_b, shift_last, wkv_fin = rwkv_branch(feat, shift_prev, wkv0, lw)
    x2d = mix_out(x2d, o_a.reshape(n, ATT_W), o_b.reshape(n, RWKV_W), ga, gb,
                  lw['proj_a'].astype(BF16), lw['proj_b'].astype(BF16), lw['w_out'].astype(BF16))
    x = x2d.reshape(B, T, D_MODEL)
    x = cross_attn(x, mk, mv, lw)
    x = x + peer(rmsnorm(x, lw['norm_ffn']), lw)
    return x, (k, v, ki, shift_last, wkv_fin)


def kernel(x_prompt, x_sample, cache_k, cache_v, cache_idx_k, state_shift, state_wkv, cache_mem_k, cache_mem_v, page_table, mem_prompt, norm_mix, w_in, rwkv_mu, rwkv_w0, rwkv_w2, rwkv_a0, rwkv_a2, rwkv_g2, rwkv_k_k, rwkv_k_a, rwkv_r_k, rwkv_ln_g, rwkv_ln_b, proj_a, proj_b, w_out, norm_ca, norm_mem, w_cq, w_mk, w_mv, w_co, norm_ffn, peer_wq, peer_k1, peer_k2, peer_u, peer_v, norm_final):
    B, S, _ = x_prompt.shape
    Bd, T, _ = x_sample.shape
    depth = w_in.shape[0]
    past = page_table.shape[1] * PAGE_SIZE
    xp, xs = x_prompt, x_sample
    st_p_all, st_s_all, mem_all = [], [], []
    for l in range(depth):
        lw = {
            'norm_mix': norm_mix[l], 'w_in_packed': _pack_w_in(w_in[l]), 'rwkv_mu': rwkv_mu[l], 'rwkv_w0': rwkv_w0[l],
            'rwkv_w2': rwkv_w2[l], 'rwkv_a0': rwkv_a0[l], 'rwkv_a2': rwkv_a2[l], 'rwkv_g2': rwkv_g2[l],
            'rwkv_k_k': rwkv_k_k[l], 'rwkv_k_a': rwkv_k_a[l], 'rwkv_r_k': rwkv_r_k[l], 'rwkv_ln_g': rwkv_ln_g[l],
            'rwkv_ln_b': rwkv_ln_b[l], 'proj_a': proj_a[l], 'proj_b': proj_b[l], 'w_out': w_out[l],
            'norm_ca': norm_ca[l], 'norm_mem': norm_mem[l], 'w_cq': w_cq[l], 'w_mk': w_mk[l], 'w_mv': w_mv[l],
            'w_co': w_co[l], 'norm_ffn': norm_ffn[l], 'peer_wq': peer_wq[l], 'peer_k1': peer_k1[l],
            'peer_k2': peer_k2[l], 'peer_u': peer_u[l], 'peer_v': peer_v[l],
        }
        mk_p, mv_p = mem_kv(mem_prompt, lw)
        xp, st_p = trunk_layer(xp, S, 0, lw, dsa_prompt,
                               jnp.zeros((B, SHIFT_W), x_prompt.dtype),
                               jnp.zeros((B, RWKV_HEADS, RWKV_N, RWKV_N), jnp.float32), mk_p, mv_p)
        attend_s = functools.partial(dsa_sample, cache_k=cache_k[l], cache_v=cache_v[l],
                                     cache_idx_k=cache_idx_k[l], page_table=page_table)
        xs, st_s = trunk_layer(xs, T, past, lw, attend_s, state_shift[l], state_wkv[l], cache_mem_k[l], cache_mem_v[l])
        st_p_all.append(st_p)
        st_s_all.append(st_s)
        mem_all.append((mk_p, mv_p))
    y_prompt = rmsnorm(xp, norm_final)
    y_sample = rmsnorm(xs, norm_final)
    stack = lambda lst, i: jnp.stack([s[i] for s in lst])
    return (y_prompt, y_sample,
            stack(st_p_all, 0), stack(st_p_all, 1), stack(st_p_all, 2), stack(st_p_all, 3), stack(st_p_all, 4),
            stack(mem_all, 0), stack(mem_all, 1),
            stack(st_s_all, 0), stack(st_s_all, 1), stack(st_s_all, 2), stack(st_s_all, 3), stack(st_s_all, 4))
```

```python
import functools
import math

import jax
import jax.numpy as jnp
import numpy as np
from jax import lax
from jax.experimental import pallas as pl
from jax.experimental.pallas import tpu as pltpu

D_MODEL = 1024
PAGE_SIZE = 128
ATT_HEADS = 8
KV_HEADS = 4
HEAD_DIM = 64
ATT_W = ATT_HEADS * HEAD_DIM
KV_W = KV_HEADS * HEAD_DIM
IDX_HEADS = 4
IDX_DIM = 64
TOPK_MAX = 256
Q_BLOCK = 128
ROPE_THETA = 10000.0
RWKV_HEADS = 8
RWKV_N = 64
RWKV_W = RWKV_HEADS * RWKV_N
LORA_W = 64
LORA_A = 64
LORA_G = 128
LN_X_EPS = 64e-5
SHIFT_W = 3 * RWKV_W + LORA_W + LORA_A + LORA_G
MEM_HEADS = 4
MEM_HD = 128
MEM_W = MEM_HEADS * MEM_HD
PEER_HEADS = 8
PEER_DK = 128
N_KEYS = 128
PEER_TOPK = 16
PEER_BLOCK = 128
NORM_EPS = 1e-6
SPLITS = (ATT_W, KV_W, KV_W, IDX_HEADS * IDX_DIM, IDX_DIM, IDX_HEADS, SHIFT_W, 2 * D_MODEL)

LANES = 128
VMEM_LIMIT = 56 * 1024 * 1024

F32 = jnp.float32
BF16 = jnp.bfloat16


def _cparams(*sem):
    return pltpu.CompilerParams(dimension_semantics=sem, vmem_limit_bytes=VMEM_LIMIT)


def _rms(x, g):
    return x * lax.rsqrt(jnp.mean(x * x, axis=-1, keepdims=True) + NORM_EPS) * g


_SEG_Q = (0, 512)
_SEG_K = (512, 768)
_SEG_V = (768, 1024)
_SEG_QI = (1024, 1280)
_SEG_KIWI = (1280, 1408)
_SEG_FEAT = (1408, 3200)
_SEG_GATE = (3200, 5248)
_PACKED_COLS = 5248


def _pack_w_in(w_in):
    pts = np.cumsum(SPLITS)[:-1].tolist()
    wq, wk, wv, wqi, wki, wwi, wfeat, wgate = jnp.split(w_in, pts, axis=-1)
    pad = jnp.zeros((w_in.shape[0], LANES - IDX_DIM - IDX_HEADS), w_in.dtype)
    return jnp.concatenate([wq, wk, wv, wqi, wki, wwi, pad, wfeat, wgate], axis=-1).astype(BF16)


def _rope_lanes(y, cos, sin_signed, first_half):
    outs = []
    for c in range(y.shape[1] // LANES):
        xc = y[:, c * LANES:(c + 1) * LANES]
        partner = jnp.where(first_half, pltpu.roll(xc, 96, 1), pltpu.roll(xc, 32, 1))
        outs.append(xc * cos + partner * sin_signed)
    return outs[0] if len(outs) == 1 else jnp.concatenate(outs, axis=1)


def _in_proj_kernel(x_ref, g_ref, w_ref, inv_ref, q_ref, k_ref, v_ref, qi_ref, kiwi_ref, feat_ref, ga_ref, gb_ref,
                    *, tm, period, offset):
    x = x_ref[...]
    h = _rms(x, g_ref[...]).astype(BF16)
    row = pl.program_id(0) * tm + lax.broadcasted_iota(jnp.int32, (tm, LANES), 0)
    pos = (row % period + offset).astype(F32)
    ang = pos * inv_ref[...]
    cos = jnp.cos(ang)
    sin = jnp.sin(ang)
    lane = lax.broadcasted_iota(jnp.int32, (tm, LANES), 1)
    first_half = (lane % HEAD_DIM) < (HEAD_DIM // 2)
    sin_signed = jnp.where(first_half, -sin, sin)

    def seg(s):
        return jnp.dot(h, w_ref[:, s[0]:s[1]], preferred_element_type=F32)

    q_ref[...] = _rope_lanes(seg(_SEG_Q), cos, sin_signed, first_half)
    k_ref[...] = _rope_lanes(seg(_SEG_K), cos, sin_signed, first_half)
    v_ref[...] = seg(_SEG_V)
    qi_ref[...] = _rope_lanes(seg(_SEG_QI), cos, sin_signed, first_half)
    kiwi = seg(_SEG_KIWI)
    kiwi_ref[...] = jnp.where(lane < IDX_DIM, _rope_lanes(kiwi, cos, sin_signed, first_half), kiwi)
    feat_ref[...] = seg(_SEG_FEAT)
    gates = jax.nn.sigmoid(seg(_SEG_GATE))
    ga_ref[...] = gates[:, :D_MODEL]
    gb_ref[...] = gates[:, D_MODEL:]


def in_proj(x2d, norm_g, w_packed, inv_tiled, *, period, offset, tm=256):
    n = x2d.shape[0]
    assert n % tm == 0
    widths = (ATT_W, KV_W, KV_W, IDX_HEADS * IDX_DIM, LANES, SHIFT_W, D_MODEL, D_MODEL)
    row = lambda w: pl.BlockSpec((tm, w), lambda i: (i, 0))
    full = lambda a: pl.BlockSpec(a.shape, lambda i: (0,) * a.ndim)
    return pl.pallas_call(
        functools.partial(_in_proj_kernel, tm=tm, period=period, offset=offset),
        grid=(n // tm,),
        in_specs=[row(D_MODEL), full(norm_g), full(w_packed), full(inv_tiled)],
        out_specs=[row(w) for w in widths],
        out_shape=[jax.ShapeDtypeStruct((n, w), F32) for w in widths],
        compiler_params=_cparams("parallel"),
        name="in_proj",
    )(x2d, norm_g, w_packed, inv_tiled)


def _mix_out_kernel(x_ref, oa_ref, ob_ref, ga_ref, gb_ref, pa_ref, pb_ref, wo_ref, o_ref):
    ya = jnp.dot(oa_ref[...].astype(BF16), pa_ref[...], preferred_element_type=F32)
    yb = jnp.dot(ob_ref[...].astype(BF16), pb_ref[...], preferred_element_type=F32)
    m = ga_ref[...] * ya + gb_ref[...] * yb
    o_ref[...] = x_ref[...] + jnp.dot(m.astype(BF16), wo_ref[...], preferred_element_type=F32)


def mix_out(x2d, oa, ob, ga, gb, pa, pb, wo, *, tm=256):
    n = x2d.shape[0]
    row = lambda w: pl.BlockSpec((tm, w), lambda i: (i, 0))
    full = lambda a: pl.BlockSpec(a.shape, lambda i: (0,) * a.ndim)
    return pl.pallas_call(
        _mix_out_kernel,
        grid=(n // tm,),
        in_specs=[row(D_MODEL), row(ATT_W), row(RWKV_W), row(D_MODEL), row(D_MODEL), full(pa), full(pb), full(wo)],
        out_specs=row(D_MODEL),
        out_shape=jax.ShapeDtypeStruct((n, D_MODEL), F32),
        compiler_params=_cparams("parallel"),
        name="mix_out",
    )(x2d, oa, ob, ga, gb, pa, pb, wo)


INT_MIN = -(2 ** 31)
NEG_BIG = -1e30
IDX_SCALE = IDX_DIM ** -0.5 * IDX_HEADS ** -0.5
KEY_NEG_INF = int(np.array(-np.inf, np.float32).view(np.int32)) ^ 0x7FFFFFFF
NO_CUT = 2 ** 30


def _sortable(s):
    b = lax.bitcast_convert_type(s, jnp.int32)
    return b ^ ((b >> 31) & 0x7FFFFFFF)


def _lane_tile(x, reps):
    return x if reps == 1 else jnp.concatenate([x] * reps, axis=1)


def _select_threshold(keys_ref, nch, cw, rows, k_top, idx_bits):
    reps = cw // LANES

    def count(pred):
        def body(c, acc):
            off = pl.multiple_of(c * cw, cw)
            kc = keys_ref[:, pl.ds(off, cw)]
            for u in range(reps):
                acc = acc + pred(kc[:, u * LANES:(u + 1) * LANES], off + u * LANES)
            return acc
        acc = lax.fori_loop(0, nch, body, jnp.zeros((rows, LANES), F32))
        return jnp.sum(acc, axis=1, keepdims=True)

    def count_ge(thr_col):
        thr = jnp.broadcast_to(thr_col, (rows, LANES))
        return count(lambda kc, off: jnp.where(kc >= thr, 1.0, 0.0))

    def value_bit(i, t_u):
        cand = t_u | jnp.left_shift(jnp.int32(1), 31 - i)
        return jnp.where(count_ge(cand ^ INT_MIN) >= k_top, cand, t_u)

    t_u = lax.fori_loop(0, 32, value_bit, jnp.zeros((rows, 1), jnp.int32))
    tau = t_u ^ INT_MIN
    n_ge = count_ge(tau)
    n_gt = count_ge(tau + 1)
    need = k_top - n_gt
    tie = jnp.logical_and(n_ge - n_gt > need, tau != KEY_NEG_INF)
    any_tie = jnp.max(jnp.where(tie, 1.0, 0.0))

    def tie_cut():
        tau_b = jnp.broadcast_to(tau, (rows, LANES))
        lane = lax.broadcasted_iota(jnp.int32, (rows, LANES), 1)

        def index_bit(i, c_lo):
            cand = c_lo | jnp.left_shift(jnp.int32(1), idx_bits - 1 - i)
            cand_b = jnp.broadcast_to(cand, (rows, LANES))
            n = count(lambda kc, off: jnp.where(kc == tau_b, jnp.where(lane + off < cand_b, 1.0, 0.0), 0.0))
            return jnp.where(n < need, cand, c_lo)

        c_lo = lax.fori_loop(0, idx_bits, index_bit, jnp.zeros((rows, 1), jnp.int32))
        return jnp.where(tie, c_lo, NO_CUT)

    cut = lax.cond(any_tie > 0.0, tie_cut, lambda: jnp.full((rows, 1), NO_CUT, jnp.int32))
    return tau, cut


def _dsa_prompt_kernel(q_ref, qi_ref, kiwi_ref, k_ref, v_ref, ki_ref, o_ref,
                       keys_ref, wb_ref, m_ref, l_ref, acc_ref, *, k_top, cw, idx_bits):
    tq = Q_BLOCK
    j = pl.program_id(1)
    nch = (j * tq + tq + cw - 1) // cw
    reps = cw // LANES
    t_b = j * tq + lax.broadcasted_iota(jnp.int32, (tq, LANES), 0)
    lane = lax.broadcasted_iota(jnp.int32, (tq, LANES), 1)

    qi = qi_ref[0]
    qi4 = jnp.concatenate([qi[:, h * IDX_DIM:(h + 1) * IDX_DIM] for h in range(IDX_HEADS)], axis=0).astype(BF16)
    wi = kiwi_ref[0][:, IDX_DIM:IDX_DIM + IDX_HEADS]
    for h in range(IDX_HEADS):
        wb_ref[h] = jnp.broadcast_to(wi[:, h:h + 1], (tq, LANES))

    def score_body(c, _):
        off = pl.multiple_of(c * cw, cw)
        kic = ki_ref[0, pl.ds(off, cw), :]
        dots = lax.dot_general(qi4, kic, (((1,), (1,)), ((), ())), preferred_element_type=F32)
        parts = []
        for u in range(reps):
            su = jnp.zeros((tq, LANES), F32)
            for h in range(IDX_HEADS):
                su = su + wb_ref[h] * jnp.maximum(dots[h * tq:(h + 1) * tq, u * LANES:(u + 1) * LANES], 0.0)
            kpos = lane + (off + u * LANES)
            parts.append(_sortable(jnp.where(kpos <= t_b, su * IDX_SCALE, -jnp.inf)))
        keys_ref[:, pl.ds(off, cw)] = _lane_tile(parts[0], 1) if reps == 1 else jnp.concatenate(parts, axis=1)
        return 0

    lax.fori_loop(0, nch, score_body, 0)

    tau, cut = _select_threshold(keys_ref, nch, cw, tq, k_top, idx_bits)
    tau_b = jnp.broadcast_to(tau, (tq, LANES))
    cut_b = jnp.broadcast_to(cut, (tq, LANES))

    q = q_ref[0] * (HEAD_DIM ** -0.5)
    qs = []
    for n in range(KV_HEADS):
        qs.append(jnp.concatenate([q[:, (2 * n) * HEAD_DIM:(2 * n + 1) * HEAD_DIM],
                                   q[:, (2 * n + 1) * HEAD_DIM:(2 * n + 2) * HEAD_DIM]], axis=0).astype(BF16))
    m_ref[...] = jnp.full(m_ref.shape, NEG_BIG, F32)
    l_ref[...] = jnp.zeros(l_ref.shape, F32)
    acc_ref[...] = jnp.zeros(acc_ref.shape, F32)

    def attn_body(c, _):
        off = pl.multiple_of(c * cw, cw)
        kc = keys_ref[:, pl.ds(off, cw)]
        bias_parts = []
        for u in range(reps):
            ku = kc[:, u * LANES:(u + 1) * LANES]
            kpos = lane + (off + u * LANES)
            tie_ok = jnp.where(ku == tau_b, jnp.where(kpos <= cut_b, 0.0, NEG_BIG), NEG_BIG)
            sel = jnp.where(ku > tau_b, 0.0, tie_ok)
            bias_parts.append(jnp.where(kpos <= t_b, sel, NEG_BIG))
        bias = bias_parts[0] if reps == 1 else jnp.concatenate(bias_parts, axis=1)
        bias2 = jnp.concatenate([bias, bias], axis=0)
        for n in range(KV_HEADS):
            kn = k_ref[0, pl.ds(off, cw), n * HEAD_DIM:(n + 1) * HEAD_DIM]
            vn = v_ref[0, pl.ds(off, cw), n * HEAD_DIM:(n + 1) * HEAD_DIM]
            s = lax.dot_general(qs[n], kn, (((1,), (1,)), ((), ())), preferred_element_type=F32) + bias2
            m_prev = m_ref[n]
            m_new = jnp.maximum(m_prev, jnp.max(s, axis=1, keepdims=True))
            p = jnp.exp(s - _lane_tile(m_new, reps))
            alpha = jnp.exp(m_prev - m_new)
            l_ref[n] = alpha * l_ref[n] + jnp.sum(p, axis=1, keepdims=True)
            acc_ref[n] = alpha[:, :HEAD_DIM] * acc_ref[n] + jnp.dot(p.astype(BF16), vn, preferred_element_type=F32)
            m_ref[n] = m_new
        return 0

    lax.fori_loop(0, nch, attn_body, 0)

    outs = []
    for n in range(KV_HEADS):
        o = acc_ref[n] / l_ref[n][:, :HEAD_DIM]
        outs += [o[:tq], o[tq:]]
    o_ref[0] = jnp.concatenate(outs, axis=1)


def dsa_prompt_attn(q, qi, kiwi, k_bf, v_bf, ki_bf, *, cw=512):
    B, S, _ = q.shape
    cw = min(cw, S)
    assert S % cw == 0 and cw % LANES == 0 and S % Q_BLOCK == 0
    k_top = min(TOPK_MAX, S // 4)
    blk = lambda w: pl.BlockSpec((1, Q_BLOCK, w), lambda b, j: (b, j, 0))
    res = lambda w: pl.BlockSpec((1, S, w), lambda b, j: (b, 0, 0), pipeline_mode=pl.Buffered(1))
    return pl.pallas_call(
        functools.partial(_dsa_prompt_kernel, k_top=k_top, cw=cw, idx_bits=max(1, (S - 1).bit_length())),
        grid=(B, S // Q_BLOCK),
        in_specs=[blk(ATT_W), blk(IDX_HEADS * IDX_DIM), blk(LANES), res(KV_W), res(KV_W), res(IDX_DIM)],
        out_specs=blk(ATT_W),
        out_shape=jax.ShapeDtypeStruct((B, S, ATT_W), F32),
        scratch_shapes=[pltpu.VMEM((Q_BLOCK, S), jnp.int32),
                        pltpu.VMEM((IDX_HEADS, Q_BLOCK, LANES), F32),
                        pltpu.VMEM((KV_HEADS, 2 * Q_BLOCK, LANES), F32),
                        pltpu.VMEM((KV_HEADS, 2 * Q_BLOCK, LANES), F32),
                        pltpu.VMEM((KV_HEADS, 2 * Q_BLOCK, HEAD_DIM), F32)],
        compiler_params=_cparams("parallel", "arbitrary"),
        name="dsa_prompt",
    )(q, qi, kiwi, k_bf, v_bf, ki_bf)


def _dsa_sample_kernel(pt_ref, q_ref, qi_ref, kiwi_ref, kn_ref, vn_ref, ck_hbm, cv_hbm, cik_hbm, o_ref,
                       kbuf, vbuf, ibuf, sems, keys_ref, m_ref, l_ref, acc_ref, *, n_pages, k_top, cw, idx_bits):
    b = pl.program_id(0)
    t_new = q_ref.shape[1]
    past = n_pages * PAGE_SIZE
    ppc = cw // PAGE_SIZE
    nch = n_pages // ppc
    reps = cw // LANES

    streams = ((cik_hbm, ibuf), (ck_hbm, kbuf), (cv_hbm, vbuf))

    def page_copy(which, p):
        src, dst = streams[which]
        return pltpu.make_async_copy(src.at[pt_ref[b, p]], dst.at[p], sems.at[which])

    def start_page(p, _):
        for which in range(len(streams)):
            page_copy(which, p).start()
        return 0

    lax.fori_loop(0, n_pages, start_page, 0)

    def wait_pages(which):
        def body(p, _):
            page_copy(which, p).wait()
            return 0
        lax.fori_loop(0, n_pages, body, 0)

    lane = lax.broadcasted_iota(jnp.int32, (t_new, LANES), 1)
    t_b = lax.broadcasted_iota(jnp.int32, (t_new, LANES), 0)
    qi = qi_ref[0]
    qi4 = jnp.concatenate([qi[:, h * IDX_DIM:(h + 1) * IDX_DIM] for h in range(IDX_HEADS)], axis=0).astype(BF16)
    kiwi = kiwi_ref[0]
    wi = kiwi[:, IDX_DIM:IDX_DIM + IDX_HEADS]
    wb = [jnp.broadcast_to(wi[:, h:h + 1], (t_new, LANES)) for h in range(IDX_HEADS)]

    def combine(dots_u):
        su = jnp.zeros((t_new, LANES), F32)
        for h in range(IDX_HEADS):
            su = su + wb[h] * jnp.maximum(dots_u[h * t_new:(h + 1) * t_new], 0.0)
        return su * IDX_SCALE

    wait_pages(0)

    def score_body(c, _):
        kic = ibuf[pl.ds(c * ppc, ppc)].reshape(cw, IDX_DIM).astype(BF16)
        dots = lax.dot_general(qi4, kic, _NT, preferred_element_type=F32)
        parts = [_sortable(combine(dots[:, u * LANES:(u + 1) * LANES])) for u in range(reps)]
        keys_ref[:, pl.ds(pl.multiple_of(c * cw, cw), cw)] = jnp.concatenate(parts, axis=1)
        return 0

    lax.fori_loop(0, nch, score_body, 0)
    ki_new = jnp.concatenate([kiwi[:, :IDX_DIM], jnp.zeros((LANES - t_new, IDX_DIM), F32)], axis=0).astype(BF16)
    dots_new = lax.dot_general(qi4, ki_new, _NT, preferred_element_type=F32)
    s_new = jnp.where(lane <= t_b, combine(dots_new), -jnp.inf)
    neg = jnp.full((t_new, LANES), -jnp.inf, F32)
    keys_ref[:, pl.ds(past, cw)] = _sortable(jnp.concatenate([s_new] + [neg] * (reps - 1), axis=1))

    tau, cut = _select_threshold(keys_ref, nch + 1, cw, t_new, k_top, idx_bits)
    tau_b = jnp.broadcast_to(tau, (t_new, LANES))
    cut_b = jnp.broadcast_to(cut, (t_new, LANES))

    q = q_ref[0] * (HEAD_DIM ** -0.5)
    qs = []
    for n in range(KV_HEADS):
        qs.append(jnp.concatenate([q[:, (2 * n) * HEAD_DIM:(2 * n + 1) * HEAD_DIM],
                                   q[:, (2 * n + 1) * HEAD_DIM:(2 * n + 2) * HEAD_DIM]], axis=0).astype(BF16))
    m_ref[...] = jnp.full(m_ref.shape, NEG_BIG, F32)
    l_ref[...] = jnp.zeros(l_ref.shape, F32)
    acc_ref[...] = jnp.zeros(acc_ref.shape, F32)

    def bias_of(kc, off, width, causal_new):
        parts = []
        for u in range(width // LANES):
            ku = kc[:, u * LANES:(u + 1) * LANES]
            kpos = lane + (off + u * LANES)
            tie_ok = jnp.where(ku == tau_b, jnp.where(kpos <= cut_b, 0.0, NEG_BIG), NEG_BIG)
            sel = jnp.where(ku > tau_b, 0.0, tie_ok)
            if causal_new:
                sel = jnp.where(lane <= t_b, sel, NEG_BIG)
            parts.append(sel)
        bias = parts[0] if len(parts) == 1 else jnp.concatenate(parts, axis=1)
        return jnp.concatenate([bias, bias], axis=0)

    def attend(kc_bf, vc_bf, bias2):
        width = bias2.shape[1]
        for n in range(KV_HEADS):
            kn = kc_bf[:, n * HEAD_DIM:(n + 1) * HEAD_DIM]
            vn = vc_bf[:, n * HEAD_DIM:(n + 1) * HEAD_DIM]
            s = lax.dot_general(qs[n], kn, _NT, preferred_element_type=F32) + bias2
            m_prev = m_ref[n]
            m_new = jnp.maximum(m_prev, jnp.max(s, axis=1, keepdims=True))
            p = jnp.exp(s - _lane_tile(m_new, width // LANES))
            alpha = jnp.exp(m_prev - m_new)
            l_ref[n] = alpha * l_ref[n] + jnp.sum(p, axis=1, keepdims=True)
            acc_ref[n] = alpha[:, :HEAD_DIM] * acc_ref[n] + jnp.dot(p.astype(BF16), vn, preferred_element_type=F32)
            m_ref[n] = m_new

    wait_pages(1)
    wait_pages(2)

    def attn_body(c, _):
        off = pl.multiple_of(c * cw, cw)
        kc = kbuf[pl.ds(c * ppc, ppc)].reshape(cw, KV_W).astype(BF16)
        vc = vbuf[pl.ds(c * ppc, ppc)].reshape(cw, KV_W).astype(BF16)
        attend(kc, vc, bias_of(keys_ref[:, pl.ds(off, cw)], off, cw, False))
        return 0

    lax.fori_loop(0, nch, attn_body, 0)
    pad_rows = jnp.zeros((LANES - t_new, KV_W), F32)
    kn_pad = jnp.concatenate([kn_ref[0], pad_rows], axis=0).astype(BF16)
    vn_pad = jnp.concatenate([vn_ref[0], pad_rows], axis=0).astype(BF16)
    attend(kn_pad, vn_pad, bias_of(keys_ref[:, pl.ds(past, LANES)], past, LANES, True))

    outs = []
    for n in range(KV_HEADS):
        o = acc_ref[n] / l_ref[n][:, :HEAD_DIM]
        outs += [o[:t_new], o[t_new:]]
    o_ref[0] = jnp.concatenate(outs, axis=1)


def dsa_sample_attn(q, qi, kiwi, k_new, v_new, cache_k, cache_v, cache_idx_k, page_table, *, cw=512):
    Bd, T, _ = q.shape
    n_pages = page_table.shape[1]
    past = n_pages * PAGE_SIZE
    cw = min(cw, past)
    assert cw % PAGE_SIZE == 0 and past % cw == 0 and T <= LANES
    k_top = min(TOPK_MAX, (past + T) // 4)
    n_phys = cache_k.shape[0]
    blk = lambda w: pl.BlockSpec((1, T, w), lambda b, pt: (b, 0, 0))
    hbm = pl.BlockSpec(memory_space=pl.ANY)
    gs = pltpu.PrefetchScalarGridSpec(
        num_scalar_prefetch=1,
        grid=(Bd,),
        in_specs=[blk(ATT_W), blk(IDX_HEADS * IDX_DIM), blk(LANES), blk(KV_W), blk(KV_W), hbm, hbm, hbm],
        out_specs=blk(ATT_W),
        scratch_shapes=[pltpu.VMEM((n_pages, PAGE_SIZE, KV_W), F32),
                        pltpu.VMEM((n_pages, PAGE_SIZE, KV_W), F32),
                        pltpu.VMEM((n_pages, PAGE_SIZE, IDX_DIM), F32),
                        pltpu.SemaphoreType.DMA((3,)),
                        pltpu.VMEM((T, past + cw), jnp.int32),
                        pltpu.VMEM((KV_HEADS, 2 * T, LANES), F32),
                        pltpu.VMEM((KV_HEADS, 2 * T, LANES), F32),
                        pltpu.VMEM((KV_HEADS, 2 * T, HEAD_DIM), F32)])
    return pl.pallas_call(
        functools.partial(_dsa_sample_kernel, n_pages=n_pages, k_top=k_top, cw=cw,
                          idx_bits=max(1, (past + cw - 1).bit_length())),
        grid_spec=gs,
        out_shape=jax.ShapeDtypeStruct((Bd, T, ATT_W), F32),
        compiler_params=_cparams("arbitrary"),
        name="dsa_sample",
    )(page_table, q, qi, kiwi, k_new, v_new,
      cache_k.reshape(n_phys, PAGE_SIZE, KV_W), cache_v.reshape(n_phys, PAGE_SIZE, KV_W), cache_idx_k)


def _cross_attn_kernel(x_ref, g_ref, wq_ref, mk_ref, mv_ref, wo_ref, o_ref):
    x = x_ref[0]
    q = jnp.dot(_rms(x, g_ref[...]).astype(BF16), wq_ref[...], preferred_element_type=F32) * (MEM_HD ** -0.5)
    q = q.astype(BF16)
    mk = mk_ref[0].astype(BF16)
    mv = mv_ref[0].astype(BF16)
    outs = []
    for h in range(MEM_HEADS):
        sl = slice(h * MEM_HD, (h + 1) * MEM_HD)
        s = lax.dot_general(q[:, sl], mk[:, sl], _NT, preferred_element_type=F32)
        p = jnp.exp(s - jnp.max(s, axis=1, keepdims=True))
        p = p / jnp.sum(p, axis=1, keepdims=True)
        outs.append(jnp.dot(p.astype(BF16), mv[:, sl], preferred_element_type=F32))
    o = jnp.concatenate(outs, axis=1).astype(BF16)
    o_ref[0] = x + jnp.dot(o, wo_ref[...], preferred_element_type=F32)


def cross_attn_mem(x, norm_g, wq_bf, mk, mv, wo_bf, *, tm=512):
    B, T, _ = x.shape
    tm = min(tm, T)
    assert T % tm == 0
    M = mk.shape[1]
    full = lambda a: pl.BlockSpec(a.shape, lambda b, i: (0,) * a.ndim)
    mem = pl.BlockSpec((1, M, MEM_W), lambda b, i: (b, 0, 0))
    return pl.pallas_call(
        _cross_attn_kernel,
        grid=(B, T // tm),
        in_specs=[pl.BlockSpec((1, tm, D_MODEL), lambda b, i: (b, i, 0)), full(norm_g), full(wq_bf), mem, mem, full(wo_bf)],
        out_specs=pl.BlockSpec((1, tm, D_MODEL), lambda b, i: (b, i, 0)),
        out_shape=jax.ShapeDtypeStruct(x.shape, F32),
        compiler_params=_cparams("parallel", "arbitrary"),
        name="cross_attn",
    )(x, norm_g, wq_bf, mk, mv, wo_bf)


def _norm_proj_kernel(x_ref, g_ref, w_ref, o_ref):
    o_ref[...] = jnp.dot(_rms(x_ref[...], g_ref[...]).astype(BF16), w_ref[...], preferred_element_type=F32)


def norm_proj(x2d, norm_g, w_bf, *, tm=256):
    n = x2d.shape[0]
    tm = min(tm, n)
    assert n % tm == 0
    return pl.pallas_call(
        _norm_proj_kernel,
        grid=(n // tm,),
        in_specs=[pl.BlockSpec((tm, x2d.shape[1]), lambda i: (i, 0)), pl.BlockSpec(norm_g.shape, lambda i: (0, 0)),
                  pl.BlockSpec(w_bf.shape, lambda i: (0, 0))],
        out_specs=pl.BlockSpec((tm, w_bf.shape[1]), lambda i: (i, 0)),
        out_shape=jax.ShapeDtypeStruct((n, w_bf.shape[1]), F32),
        compiler_params=_cparams("parallel"),
        name="norm_proj",
    )(x2d, norm_g, w_bf)


PEER_HALF = PEER_DK // 2
PEER_TE = 256


def _extract_top(work_ref, n_keys, tn, on_pick):
    sub = lax.broadcasted_iota(jnp.int32, (n_keys, tn), 0).astype(F32)
    for a in range(PEER_TOPK):
        w = work_ref[...]
        m = jnp.max(w, axis=0, keepdims=True)
        idx = jnp.min(jnp.where(w == m, sub, float(n_keys)), axis=0, keepdims=True)
        hit = sub == idx
        work_ref[...] = jnp.where(hit, -jnp.inf, w)
        on_pick(a, m, hit)


def _peer_kernel(x_ref, g_ref, gf_ref, wq_ref, k1_ref, k2_ref, u_ref, vt_ref, o_ref,
                 xn_ref, s_ref, rank_ref, vals_ref, work_ref, cand_ref, c_ref, l_ref, e2_ref, acc_ref,
                 *, tn, final_norm):
    e = pl.program_id(1)
    n_e = pl.num_programs(1)

    @pl.when(e == 0)
    def _route():
        xn = _rms(x_ref[...], g_ref[...]).astype(BF16)
        xn_ref[...] = xn
        q = jnp.dot(xn, wq_ref[...], preferred_element_type=F32).astype(BF16)
        for h in range(PEER_HEADS):
            q1 = q[:, h * PEER_DK:h * PEER_DK + PEER_HALF]
            q2 = q[:, h * PEER_DK + PEER_HALF:(h + 1) * PEER_DK]
            nt = (((1,), (1,)), ((), ()))
            s_ref[h] = lax.dot_general(k1_ref[h], q1, nt, preferred_element_type=F32)
            s_ref[PEER_HEADS + h] = lax.dot_general(k2_ref[h], q2, nt, preferred_element_type=F32)

        def top_keys(hh, _):
            work_ref[...] = s_ref[hh]
            rank_ref[hh] = jnp.full((N_KEYS, tn), float(PEER_TOPK), F32)

            def pick(a, m, hit):
                vals_ref[hh, a:a + 1, :] = m
                rank_ref[hh] = jnp.where(hit, float(a), rank_ref[hh])
            _extract_top(work_ref, N_KEYS, tn, pick)
            return 0

        lax.fori_loop(0, 2 * PEER_HEADS, top_keys, 0)

        def gates(h, _):
            v1 = vals_ref[h]
            v2 = vals_ref[PEER_HEADS + h]
            for a in range(PEER_TOPK):
                cand_ref[a * PEER_TOPK:(a + 1) * PEER_TOPK, :] = v1[a:a + 1, :] + v2
            _extract_top(cand_ref, PEER_TOPK * PEER_TOPK, tn, lambda a, m, hit: None)
            e1 = jnp.exp(v1 - v1[0:1, :])
            e2 = jnp.exp(v2 - v2[0:1, :])
            z = jnp.zeros((1, tn), F32)
            r1 = rank_ref[h]
            lfull = jnp.zeros((N_KEYS, tn), F32)
            for a in range(PEER_TOPK):
                sel = jnp.where(cand_ref[a * PEER_TOPK:(a + 1) * PEER_TOPK, :] == -jnp.inf, 1.0, 0.0)
                z = z + e1[a:a + 1, :] * jnp.sum(sel * e2, axis=0, keepdims=True)
                lfull = lfull + jnp.where(r1 == float(a), jnp.sum(sel, axis=0, keepdims=True), 0.0)
            l_ref[h] = lfull
            c_ref[h] = jnp.where(r1 < float(PEER_TOPK), jnp.exp(s_ref[h] - v1[0:1, :]) / z, 0.0)
            e2_ref[h] = jnp.exp(s_ref[PEER_HEADS + h] - v2[0:1, :])
            return 0

        lax.fori_loop(0, PEER_HEADS, gates, 0)
        acc_ref[...] = jnp.zeros(acc_ref.shape, F32)

    ht = lax.dot_general(u_ref[...], xn_ref[...], (((1,), (1,)), ((), ())), preferred_element_type=F32)
    parts = []
    for ii in range(PEER_TE // N_KEYS):
        i = e * (PEER_TE // N_KEYS) + ii
        g = jnp.zeros((N_KEYS, tn), F32)
        for h in range(PEER_HEADS):
            cb = c_ref[h, pl.ds(i, 1), :]
            lb = l_ref[h, pl.ds(i, 1), :]
            g = g + cb * jnp.where(rank_ref[PEER_HEADS + h] < lb, e2_ref[h], 0.0)
        hi = ht[ii * N_KEYS:(ii + 1) * N_KEYS]
        act = 0.5 * hi * (1.0 + lax.erf(hi * (2.0 ** -0.5)))
        parts.append((g * act).astype(BF16))
    w = jnp.concatenate(parts, axis=0)
    acc_ref[...] += jnp.dot(vt_ref[...], w, preferred_element_type=F32)

    @pl.when(e == n_e - 1)
    def _finish():
        y = x_ref[...] + acc_ref[...].T
        if final_norm:
            y = _rms(y, gf_ref[...])
        o_ref[...] = y


def peer_ffn(x2d, norm_g, final_g, wq_bf, k1_bf, k2_bf, u_bf, vt_bf, *, final_norm, tn=512):
    n = x2d.shape[0]
    tn = min(tn, n)
    assert n % tn == 0 and tn % LANES == 0
    n_exp = u_bf.shape[0]
    full = lambda a: pl.BlockSpec(a.shape, lambda t, e: (0,) * a.ndim)
    return pl.pallas_call(
        functools.partial(_peer_kernel, tn=tn, final_norm=final_norm),
        grid=(n // tn, n_exp // PEER_TE),
        in_specs=[pl.BlockSpec((tn, D_MODEL), lambda t, e: (t, 0)), full(norm_g), full(final_g), full(wq_bf),
                  full(k1_bf), full(k2_bf),
                  pl.BlockSpec((PEER_TE, D_MODEL), lambda t, e: (e, 0)),
                  pl.BlockSpec((D_MODEL, PEER_TE), lambda t, e: (0, e))],
        out_specs=pl.BlockSpec((tn, D_MODEL), lambda t, e: (t, 0)),
        out_shape=jax.ShapeDtypeStruct((n, D_MODEL), F32),
        scratch_shapes=[pltpu.VMEM((tn, D_MODEL), BF16),
                        pltpu.VMEM((2 * PEER_HEADS, N_KEYS, tn), F32),
                        pltpu.VMEM((2 * PEER_HEADS, N_KEYS, tn), F32),
                        pltpu.VMEM((2 * PEER_HEADS, PEER_TOPK, tn), F32),
                        pltpu.VMEM((N_KEYS, tn), F32),
                        pltpu.VMEM((PEER_TOPK * PEER_TOPK, tn), F32),
                        pltpu.VMEM((PEER_HEADS, N_KEYS, tn), F32),
                        pltpu.VMEM((PEER_HEADS, N_KEYS, tn), F32),
                        pltpu.VMEM((PEER_HEADS, N_KEYS, tn), F32),
                        pltpu.VMEM((D_MODEL, tn), F32)],
        compiler_params=_cparams("parallel", "arbitrary"),
        name="peer_ffn",
    )(x2d, norm_g, final_g, wq_bf, k1_bf, k2_bf, u_bf, vt_bf)


RWKV_GROUP = 4
RWKV_GW = RWKV_GROUP * RWKV_N
RWKV_CHUNK = 64


def _split_bf16(x, terms):
    parts = []
    for _ in range(terms):
        p = x.astype(BF16)
        parts.append(p)
        x = x - p.astype(F32)
    return parts


def _dot3(a, b, dims=(((1,), (0,)), ((), ()))):
    ah, al = _split_bf16(a, 2)
    bh, bl = _split_bf16(b, 2)
    d = lambda x, y: lax.dot_general(x, y, dims, preferred_element_type=F32)
    return d(ah, bh) + d(al, bh) + d(ah, bl)


def _dot_exact_rhs(a, b_exact_bf16):
    return sum(jnp.dot(p, b_exact_bf16, preferred_element_type=F32) for p in _split_bf16(a, 3))


def _dot_exact_lhs(a_exact_bf16, b):
    return sum(jnp.dot(a_exact_bf16, p, preferred_element_type=F32) for p in _split_bf16(b, 3))


_NT = (((1,), (1,)), ((), ()))
_TN = (((0,), (0,)), ((), ()))


def _rwkv_kernel(feat_ref, prev_ref, s0t_ref, mu_ref, w0_ref, w2_ref, a0_ref, a2_ref, g2_ref, kk_ref, ka_ref, rk_ref,
                 lng_ref, lnb_ref, hsum_ref, tri_ref, o_ref, sfin_ref, state_ref, carry_ref, *, c_len, n_double):
    c = pl.program_id(1)
    n_c = pl.num_programs(1)
    n4 = RWKV_GROUP * c_len

    @pl.when(c == 0)
    def _init():
        carry_ref[0:1, :] = prev_ref[0]
        state_ref[...] = jnp.zeros(state_ref.shape, F32)
        for h in range(RWKV_HEADS):
            g, hh = divmod(h, RWKV_GROUP)
            state_ref[g, hh * RWKV_N:(hh + 1) * RWKV_N, hh * RWKV_N:(hh + 1) * RWKV_N] = s0t_ref[0, h]

    feat = feat_ref[0]
    row = lax.broadcasted_iota(jnp.int32, feat.shape, 0)
    shifted = jnp.where(row == 0, jnp.broadcast_to(carry_ref[0:1, :], feat.shape), pltpu.roll(feat, 1, 0))
    carry_ref[0:1, :] = feat[c_len - 1:c_len, :]
    mixed = feat + (shifted - feat) * mu_ref[...]
    r = mixed[:, 0:RWKV_W]
    k = mixed[:, RWKV_W:2 * RWKV_W]
    v = mixed[:, 2 * RWKV_W:3 * RWKV_W]
    o1 = 3 * RWKV_W
    wd = mixed[:, o1:o1 + LORA_W]
    ad = mixed[:, o1 + LORA_W:o1 + LORA_W + LORA_A]
    gd = mixed[:, o1 + LORA_W + LORA_A:]

    zw = -(w0_ref[...] + _dot3(jnp.tanh(wd), w2_ref[...]))
    log_w = -(jnp.maximum(zw, 0.0) + jnp.log1p(jnp.exp(-jnp.abs(zw)))) - 0.5
    dlog = -jnp.exp(log_w)
    a = jax.nn.sigmoid(a0_ref[...] + _dot3(ad, a2_ref[...]))
    gate = _dot3(jax.nn.sigmoid(gd), g2_ref[...])
    hsum = hsum_ref[...]
    kk = k * kk_ref[...]
    kk = kk / jnp.maximum(jnp.sqrt(_dot_exact_rhs(kk * kk, hsum)), 1e-12)
    k_eff = k * (1.0 + (a - 1.0) * ka_ref[...])

    cum = _dot_exact_lhs(tri_ref[...], dlog)
    g_in = jnp.exp(cum)
    g_inv = jnp.exp(-cum)
    kk_t = kk * jnp.exp(cum - dlog)
    b_h = kk * a * g_inv
    k_h = k_eff * g_inv
    r_t = r * g_in
    g_last = g_in[c_len - 1:c_len, :]

    rr = lax.broadcasted_iota(jnp.int32, (n4, n4), 0)
    cc = lax.broadcasted_iota(jnp.int32, (n4, n4), 1)
    eye = jnp.where(rr == cc, 1.0, 0.0)
    lane_g = lax.broadcasted_iota(jnp.int32, (c_len, RWKV_GW), 1) // RWKV_N
    rr_g = lax.broadcasted_iota(jnp.int32, (RWKV_GW, RWKV_GW), 0)
    cc_g = lax.broadcasted_iota(jnp.int32, (RWKV_GW, RWKV_GW), 1)

    ys = []
    for g in range(RWKV_HEADS // RWKV_GROUP):
        sl = slice(g * RWKV_GW, (g + 1) * RWKV_GW)

        def stack(x):
            xg = x[:, sl]
            return jnp.concatenate([jnp.where(lane_g == h, xg, 0.0) for h in range(RWKV_GROUP)], axis=0)

        kks, rs, bs, ks, vs = stack(kk_t), stack(r_t), stack(b_h), stack(k_h), stack(v)
        l_b = jnp.where(rr > cc, _dot3(kks, bs, _NT), 0.0)
        l_k = jnp.where(rr > cc, _dot3(kks, ks, _NT), 0.0)
        m_b = jnp.where(rr >= cc, _dot3(rs, bs, _NT), 0.0)
        m_k = jnp.where(rr >= cc, _dot3(rs, ks, _NT), 0.0)
        npow = -l_b
        t_inv = eye + npow
        for _ in range(n_double):
            npow = _dot3(npow, npow)
            t_inv = t_inv + _dot3(t_inv, npow)
        w1 = -_dot3(t_inv, kks)
        u1 = -_dot3(t_inv, _dot3(l_k, vs))
        gl = g_last[:, sl]
        bsg = bs * gl
        ksg = ks * gl
        g_mat = jnp.where(rr_g == cc_g, jnp.broadcast_to(gl, (RWKV_GW, RWKV_GW)), 0.0) + _dot3(bsg, w1, _TN)
        h_mat = _dot3(bsg, u1, _TN) + _dot3(ksg, vs, _TN)
        r_y = rs + _dot3(m_b, w1)
        y_0 = _dot3(m_b, u1) + _dot3(m_k, vs)
        s_t = state_ref[g]
        y_st = _dot3(r_y, s_t) + y_0
        state_ref[g] = _dot3(g_mat, s_t) + h_mat
        y = y_st[0:c_len]
        for h in range(1, RWKV_GROUP):
            y = y + y_st[h * c_len:(h + 1) * c_len]
        ys.append(y)
    y = jnp.concatenate(ys, axis=1)

    inv_n = 1.0 / RWKV_N
    mean = _dot_exact_rhs(y, hsum) * inv_n
    yc = y - mean
    var = _dot_exact_rhs(yc * yc, hsum) * inv_n
    yn = yc * lax.rsqrt(var + LN_X_EPS) * lng_ref[...] + lnb_ref[...]
    bonus = _dot_exact_rhs(r * k_eff * rk_ref[...], hsum) * v
    o_ref[0] = (yn + bonus) * gate

    @pl.when(c == n_c - 1)
    def _fin():
        for h in range(RWKV_HEADS):
            g, hh = divmod(h, RWKV_GROUP)
            sfin_ref[0, h] = state_ref[g, hh * RWKV_N:(hh + 1) * RWKV_N, hh * RWKV_N:(hh + 1) * RWKV_N]


def rwkv_mix(feat, prev, s0, lw, *, c_len):
    B, T, _ = feat.shape
    assert T % c_len == 0 and c_len % 8 == 0
    n_double = max(0, (c_len - 1).bit_length() - 1)
    hsum = (jnp.arange(RWKV_W)[:, None] // RWKV_N == jnp.arange(RWKV_W)[None, :] // RWKV_N).astype(BF16)
    tri = (jnp.arange(c_len)[:, None] >= jnp.arange(c_len)[None, :]).astype(BF16)
    row = lambda a: a.reshape(1, -1).astype(F32)
    params = [row(lw['rwkv_mu']), row(lw['rwkv_w0']), lw['rwkv_w2'], row(lw['rwkv_a0']), lw['rwkv_a2'], lw['rwkv_g2'],
              row(lw['rwkv_k_k']), row(lw['rwkv_k_a']), row(lw['rwkv_r_k']), row(lw['rwkv_ln_g']), row(lw['rwkv_ln_b']),
              hsum, tri]
    full = lambda a: pl.BlockSpec(a.shape, lambda b, c: (0,) * a.ndim)
    st_spec = pl.BlockSpec((1, RWKV_HEADS, RWKV_N, RWKV_N), lambda b, c: (b, 0, 0, 0))
    out, s_fin_t = pl.pallas_call(
        functools.partial(_rwkv_kernel, c_len=c_len, n_double=n_double),
        grid=(B, T // c_len),
        in_specs=[pl.BlockSpec((1, c_len, SHIFT_W), lambda b, c: (b, c, 0)),
                  pl.BlockSpec((1, 1, SHIFT_W), lambda b, c: (b, 0, 0)), st_spec] + [full(p) for p in params],
        out_specs=[pl.BlockSpec((1, c_len, RWKV_W), lambda b, c: (b, c, 0)), st_spec],
        out_shape=[jax.ShapeDtypeStruct((B, T, RWKV_W), F32),
                   jax.ShapeDtypeStruct((B, RWKV_HEADS, RWKV_N, RWKV_N), F32)],
        scratch_shapes=[pltpu.VMEM((RWKV_HEADS // RWKV_GROUP, RWKV_GW, RWKV_GW), F32),
                        pltpu.VMEM((8, SHIFT_W), F32)],
        compiler_params=_cparams("parallel", "arbitrary"),
        name="rwkv_mix",
    )(feat, prev[:, None, :], jnp.swapaxes(s0, -1, -2), *params)
    return out, jnp.swapaxes(s_fin_t, -1, -2)


def _rope_inv_tiled():
    half = HEAD_DIM // 2
    inv = ROPE_THETA ** (-jnp.arange(half, dtype=jnp.float32) / half)
    return jnp.tile(inv, LANES // half)[None, :]


def trunk_layer(x, period, offset, lw, paged, shift_prev, wkv0, mk, mv, final_g, is_last):
    B, T, _ = x.shape
    n = B * T
    x2d = x.reshape(n, D_MODEL)
    q, k, v, qi, kiwi, feat, ga, gb = in_proj(x2d, lw['norm_mix'], lw['w_in_packed'], _rope_inv_tiled(),
                                              period=period, offset=offset)
    b3 = lambda a: a.reshape(B, T, a.shape[-1])
    ki = kiwi[:, :IDX_DIM].reshape(B, T, IDX_DIM)
    feat = b3(feat)
    if paged is None:
        o_a = dsa_prompt_attn(b3(q), b3(qi), b3(kiwi), b3(k).astype(BF16), b3(v).astype(BF16), ki.astype(BF16))
    else:
        o_a = dsa_sample_attn(b3(q), b3(qi), b3(kiwi), b3(k), b3(v), *paged)
    o_b, wkv_fin = rwkv_mix(feat, shift_prev, wkv0, lw, c_len=min(T, RWKV_CHUNK))
    shift_last = feat[:, -1]
    x2d = mix_out(x2d, o_a.reshape(n, ATT_W), o_b.reshape(n, RWKV_W), ga, gb, lw['proj_a'], lw['proj_b'], lw['w_out'])
    x = cross_attn_mem(x2d.reshape(B, T, D_MODEL), lw['norm_ca'], lw['w_cq'], mk, mv, lw['w_co'])
    x2d = peer_ffn(x.reshape(n, D_MODEL), lw['norm_ffn'], final_g, lw['peer_wq'], lw['peer_k1'],
                   lw['peer_k2'], lw['peer_u'], lw['peer_vt'], final_norm=is_last)
    new_k = k.reshape(B, T, KV_HEADS, HEAD_DIM)
    new_v = v.reshape(B, T, KV_HEADS, HEAD_DIM)
    return x2d.reshape(B, T, D_MODEL), (new_k, new_v, ki, shift_last, wkv_fin)


def kernel(x_prompt, x_sample, cache_k, cache_v, cache_idx_k, state_shift, state_wkv, cache_mem_k, cache_mem_v, page_table, mem_prompt, norm_mix, w_in, rwkv_mu, rwkv_w0, rwkv_w2, rwkv_a0, rwkv_a2, rwkv_g2, rwkv_k_k, rwkv_k_a, rwkv_r_k, rwkv_ln_g, rwkv_ln_b, proj_a, proj_b, w_out, norm_ca, norm_mem, w_cq, w_mk, w_mv, w_co, norm_ffn, peer_wq, peer_k1, peer_k2, peer_u, peer_v, norm_final):
    B, S, _ = x_prompt.shape
    Bd, T, _ = x_sample.shape
    depth = w_in.shape[0]
    past = page_table.shape[1] * PAGE_SIZE
    xp, xs = x_prompt, x_sample
    st_p_all, st_s_all, mem_all = [], [], []
    bf = lambda a: a.astype(BF16)
    row = lambda a: a.reshape(1, -1)
    final_g = row(norm_final)
    M = mem_prompt.shape[1]
    for l in range(depth):
        lw = {
            'norm_mix': row(norm_mix[l]), 'w_in_packed': _pack_w_in(w_in[l]), 'rwkv_mu': rwkv_mu[l], 'rwkv_w0': rwkv_w0[l],
            'rwkv_w2': rwkv_w2[l], 'rwkv_a0': rwkv_a0[l], 'rwkv_a2': rwkv_a2[l], 'rwkv_g2': rwkv_g2[l],
            'rwkv_k_k': rwkv_k_k[l], 'rwkv_k_a': rwkv_k_a[l], 'rwkv_r_k': rwkv_r_k[l], 'rwkv_ln_g': rwkv_ln_g[l],
            'rwkv_ln_b': rwkv_ln_b[l], 'proj_a': bf(proj_a[l]), 'proj_b': bf(proj_b[l]), 'w_out': bf(w_out[l]),
            'norm_ca': row(norm_ca[l]), 'w_cq': bf(w_cq[l]), 'w_co': bf(w_co[l]), 'norm_ffn': row(norm_ffn[l]),
            'peer_wq': bf(peer_wq[l]), 'peer_k1': bf(peer_k1[l]), 'peer_k2': bf(peer_k2[l]),
            'peer_u': bf(peer_u[l]), 'peer_vt': bf(peer_v[l].T),
        }
        is_last = l == depth - 1
        mem_kv = norm_proj(mem_prompt.reshape(B * M, D_MODEL), row(norm_mem[l]),
                           bf(jnp.concatenate([w_mk[l], w_mv[l]], axis=1)))
        mk_p = mem_kv[:, :MEM_W].reshape(B, M, MEM_W)
        mv_p = mem_kv[:, MEM_W:].reshape(B, M, MEM_W)
        xp, st_p = trunk_layer(xp, S, 0, lw, None,
                               jnp.zeros((B, SHIFT_W), F32), jnp.zeros((B, RWKV_HEADS, RWKV_N, RWKV_N), F32),
                               mk_p, mv_p, final_g, is_last)
        paged = (cache_k[l], cache_v[l], cache_idx_k[l], page_table)
        xs, st_s = trunk_layer(xs, T, past, lw, paged, state_shift[l], state_wkv[l],
                               cache_mem_k[l].reshape(Bd, M, MEM_W), cache_mem_v[l].reshape(Bd, M, MEM_W), final_g, is_last)
        st_p_all.append(st_p)
        st_s_all.append(st_s)
        mem_all.append((mk_p.reshape(B, M, MEM_HEADS, MEM_HD), mv_p.reshape(B, M, MEM_HEADS, MEM_HD)))
    y_prompt, y_sample = xp, xs
    stack = lambda lst, i: jnp.stack([s[i] for s in lst])
    return (y_prompt, y_sample,
            stack(st_p_all, 0), stack(st_p_all, 1), stack(st_p_all, 2), stack(st_p_all, 3), stack(st_p_all, 4),
            stack(mem_all, 0), stack(mem_all, 1),
            stack(st_s_all, 0), stack(st_s_all, 1), stack(st_s_all, 2), stack(st_s_all, 3), stack(st_s_all, 4))
```

```python
import functools
import math

import jax
import jax.numpy as jnp
import numpy as np
from jax import lax
from jax.experimental import pallas as pl
from jax.experimental.pallas import tpu as pltpu

D_MODEL = 1024
PAGE_SIZE = 128
ATT_HEADS = 8
KV_HEADS = 4
HEAD_DIM = 64
ATT_W = ATT_HEADS * HEAD_DIM
KV_W = KV_HEADS * HEAD_DIM
IDX_HEADS = 4
IDX_DIM = 64
TOPK_MAX = 256
Q_BLOCK = 128
ROPE_THETA = 10000.0
RWKV_HEADS = 8
RWKV_N = 64
RWKV_W = RWKV_HEADS * RWKV_N
LORA_W = 64
LORA_A = 64
LORA_G = 128
LN_X_EPS = 64e-5
SHIFT_W = 3 * RWKV_W + LORA_W + LORA_A + LORA_G
MEM_HEADS = 4
MEM_HD = 128
MEM_W = MEM_HEADS * MEM_HD
PEER_HEADS = 8
PEER_DK = 128
N_KEYS = 128
PEER_TOPK = 16
PEER_BLOCK = 128
NORM_EPS = 1e-6
SPLITS = (ATT_W, KV_W, KV_W, IDX_HEADS * IDX_DIM, IDX_DIM, IDX_HEADS, SHIFT_W, 2 * D_MODEL)

LANES = 128
VMEM_LIMIT = 56 * 1024 * 1024

F32 = jnp.float32
BF16 = jnp.bfloat16


def _cparams(*sem):
    return pltpu.CompilerParams(dimension_semantics=sem, vmem_limit_bytes=VMEM_LIMIT)


def _rms(x, g):
    return x * lax.rsqrt(jnp.mean(x * x, axis=-1, keepdims=True) + NORM_EPS) * g


_SEG_Q = (0, 512)
_SEG_K = (512, 768)
_SEG_V = (768, 1024)
_SEG_QI = (1024, 1280)
_SEG_KIWI = (1280, 1408)
_SEG_FEAT = (1408, 3200)
_SEG_GATE = (3200, 5248)
_SEG_VAUG = (5248, 5760)
_PACKED_COLS = 5760


def _pack_w_in(w_in):
    pts = np.cumsum(SPLITS)[:-1].tolist()
    wq, wk, wv, wqi, wki, wwi, wfeat, wgate = jnp.split(w_in, pts, axis=-1)
    d = w_in.shape[0]
    pad = jnp.zeros((d, LANES - IDX_DIM - IDX_HEADS), w_in.dtype)
    wv_aug = jnp.concatenate([wv.reshape(d, KV_HEADS, HEAD_DIM), jnp.zeros((d, KV_HEADS, HEAD_DIM), w_in.dtype)],
                             axis=-1).reshape(d, 2 * KV_W)
    return jnp.concatenate([wq, wk, wv, wqi, wki, wwi, pad, wfeat, wgate, wv_aug], axis=-1).astype(BF16)


def _rope_lanes(y, cos, sin_signed, first_half):
    outs = []
    for c in range(y.shape[1] // LANES):
        xc = y[:, c * LANES:(c + 1) * LANES]
        partner = jnp.where(first_half, pltpu.roll(xc, 96, 1), pltpu.roll(xc, 32, 1))
        outs.append(xc * cos + partner * sin_signed)
    return outs[0] if len(outs) == 1 else jnp.concatenate(outs, axis=1)


def _in_proj_kernel(x_ref, g_ref, w_ref, inv_ref, q_ref, k_ref, v_ref, qi_ref, kiwi_ref, feat_ref, ga_ref, gb_ref,
                    *bf_refs, tm, period, offset):
    x = x_ref[...]
    h = _rms(x, g_ref[...]).astype(BF16)
    row = pl.program_id(0) * tm + lax.broadcasted_iota(jnp.int32, (tm, LANES), 0)
    pos = (row % period + offset).astype(F32)
    ang = pos * inv_ref[...]
    cos = jnp.cos(ang)
    sin = jnp.sin(ang)
    lane = lax.broadcasted_iota(jnp.int32, (tm, LANES), 1)
    first_half = (lane % HEAD_DIM) < (HEAD_DIM // 2)
    sin_signed = jnp.where(first_half, -sin, sin)

    def seg(s):
        return jnp.dot(h, w_ref[:, s[0]:s[1]], preferred_element_type=F32)

    q_ref[...] = _rope_lanes(seg(_SEG_Q), cos, sin_signed, first_half)
    k = _rope_lanes(seg(_SEG_K), cos, sin_signed, first_half)
    k_ref[...] = k
    v_ref[...] = seg(_SEG_V)
    qi_ref[...] = _rope_lanes(seg(_SEG_QI), cos, sin_signed, first_half)
    kiwi = seg(_SEG_KIWI)
    kiwi = jnp.where(lane < IDX_DIM, _rope_lanes(kiwi, cos, sin_signed, first_half), kiwi)
    kiwi_ref[...] = kiwi
    feat_ref[...] = seg(_SEG_FEAT)
    gates = jax.nn.sigmoid(seg(_SEG_GATE))
    ga_ref[...] = gates[:, :D_MODEL]
    gb_ref[...] = gates[:, D_MODEL:]
    if bf_refs:
        kbf_ref, kibf_ref, vaug_ref = bf_refs
        kbf_ref[...] = k.astype(BF16)
        kibf_ref[...] = kiwi[:, :IDX_DIM].astype(BF16)
        ones_half = jnp.where(lane >= HEAD_DIM, 1.0, 0.0)
        vaug_ref[...] = (seg(_SEG_VAUG) + jnp.concatenate([ones_half] * (2 * KV_W // LANES), axis=1)).astype(BF16)


def in_proj(x2d, norm_g, w_packed, inv_tiled, *, period, offset, attn_operands=False, tm=256):
    n = x2d.shape[0]
    assert n % tm == 0
    outs = [(w, F32) for w in (ATT_W, KV_W, KV_W, IDX_HEADS * IDX_DIM, LANES, SHIFT_W, D_MODEL, D_MODEL)]
    if attn_operands:
        outs += [(KV_W, BF16), (IDX_DIM, BF16), (2 * KV_W, BF16)]
    row = lambda w: pl.BlockSpec((tm, w), lambda i: (i, 0))
    full = lambda a: pl.BlockSpec(a.shape, lambda i: (0,) * a.ndim)
    return pl.pallas_call(
        functools.partial(_in_proj_kernel, tm=tm, period=period, offset=offset),
        grid=(n // tm,),
        in_specs=[row(D_MODEL), full(norm_g), full(w_packed), full(inv_tiled)],
        out_specs=[row(w) for w, _ in outs],
        out_shape=[jax.ShapeDtypeStruct((n, w), dt) for w, dt in outs],
        compiler_params=_cparams("parallel"),
        name="in_proj",
    )(x2d, norm_g, w_packed, inv_tiled)


def _mix_out_kernel(x_ref, oa_ref, ob_ref, ga_ref, gb_ref, pa_ref, pb_ref, wo_ref, o_ref):
    ya = jnp.dot(oa_ref[...].astype(BF16), pa_ref[...], preferred_element_type=F32)
    yb = jnp.dot(ob_ref[...].astype(BF16), pb_ref[...], preferred_element_type=F32)
    m = ga_ref[...] * ya + gb_ref[...] * yb
    o_ref[...] = x_ref[...] + jnp.dot(m.astype(BF16), wo_ref[...], preferred_element_type=F32)


def mix_out(x2d, oa, ob, ga, gb, pa, pb, wo, *, tm=256):
    n = x2d.shape[0]
    row = lambda w: pl.BlockSpec((tm, w), lambda i: (i, 0))
    full = lambda a: pl.BlockSpec(a.shape, lambda i: (0,) * a.ndim)
    return pl.pallas_call(
        _mix_out_kernel,
        grid=(n // tm,),
        in_specs=[row(D_MODEL), row(ATT_W), row(RWKV_W), row(D_MODEL), row(D_MODEL), full(pa), full(pb), full(wo)],
        out_specs=row(D_MODEL),
        out_shape=jax.ShapeDtypeStruct((n, D_MODEL), F32),
        compiler_params=_cparams("parallel"),
        name="mix_out",
    )(x2d, oa, ob, ga, gb, pa, pb, wo)


LOG2_E = math.log2(math.e)
INT_MIN = -(2 ** 31)
NEG_BIG = -1e30
IDX_SCALE = IDX_DIM ** -0.5 * IDX_HEADS ** -0.5
KEY_NEG_INF = int(np.array(-np.inf, np.float32).view(np.int32)) ^ 0x7FFFFFFF
NO_CUT = 2 ** 30


def _sortable(s):
    b = lax.bitcast_convert_type(s, jnp.int32)
    return b ^ ((b >> 31) & 0x7FFFFFFF)


def _lane_tile(x, reps):
    return x if reps == 1 else jnp.concatenate([x] * reps, axis=1)


def _select_threshold(keys_ref, nch, cw, rows, k_top, idx_bits):
    reps = cw // LANES

    def count(pred):
        def body(c, acc):
            off = pl.multiple_of(c * cw, cw)
            kc = keys_ref[:, pl.ds(off, cw)]
            for u in range(reps):
                acc = acc + pred(kc[:, u * LANES:(u + 1) * LANES], off + u * LANES)
            return acc
        acc = lax.fori_loop(0, nch, body, jnp.zeros((rows, LANES), F32))
        return jnp.sum(acc, axis=1, keepdims=True)

    def count_ge(thr_col):
        thr = jnp.broadcast_to(thr_col, (rows, LANES))
        return count(lambda kc, off: jnp.where(kc >= thr, 1.0, 0.0))

    def value_bit(i, t_u):
        cand = t_u | jnp.left_shift(jnp.int32(1), 31 - i)
        return jnp.where(count_ge(cand ^ INT_MIN) >= k_top, cand, t_u)

    t_u = lax.fori_loop(0, 32, value_bit, jnp.zeros((rows, 1), jnp.int32))
    tau = t_u ^ INT_MIN
    n_ge = count_ge(tau)
    n_gt = count_ge(tau + 1)
    need = k_top - n_gt
    tie = jnp.logical_and(n_ge - n_gt > need, tau != KEY_NEG_INF)
    any_tie = jnp.max(jnp.where(tie, 1.0, 0.0))

    def tie_cut():
        tau_b = jnp.broadcast_to(tau, (rows, LANES))
        lane = lax.broadcasted_iota(jnp.int32, (rows, LANES), 1)

        def index_bit(i, c_lo):
            cand = c_lo | jnp.left_shift(jnp.int32(1), idx_bits - 1 - i)
            cand_b = jnp.broadcast_to(cand, (rows, LANES))
            n = count(lambda kc, off: jnp.where(kc == tau_b, jnp.where(lane + off < cand_b, 1.0, 0.0), 0.0))
            return jnp.where(n < need, cand, c_lo)

        c_lo = lax.fori_loop(0, idx_bits, index_bit, jnp.zeros((rows, 1), jnp.int32))
        return jnp.where(tie, c_lo, NO_CUT)

    cut = lax.cond(any_tie > 0.0, tie_cut, lambda: jnp.full((rows, 1), NO_CUT, jnp.int32))
    return tau, cut


def _dsa_prompt_kernel(q_ref, qi_ref, kiwi_ref, k_ref, v_ref, ki_ref, o_ref,
                       keys_ref, wb_ref, m_ref, acc_ref, *, k_top, cw, idx_bits):
    tq = Q_BLOCK
    j = pl.program_id(1)
    nch = (j * tq + tq + cw - 1) // cw
    reps = cw // LANES
    t_b = j * tq + lax.broadcasted_iota(jnp.int32, (tq, LANES), 0)
    lane = lax.broadcasted_iota(jnp.int32, (tq, LANES), 1)

    qi = qi_ref[0]
    qi4 = jnp.concatenate([qi[:, h * IDX_DIM:(h + 1) * IDX_DIM] for h in range(IDX_HEADS)], axis=0).astype(BF16)
    wi = kiwi_ref[0][:, IDX_DIM:IDX_DIM + IDX_HEADS]
    for h in range(IDX_HEADS):
        wb_ref[h] = jnp.broadcast_to(wi[:, h:h + 1], (tq, LANES))

    def score_body(c, _):
        off = pl.multiple_of(c * cw, cw)
        kic = ki_ref[0, pl.ds(off, cw), :]
        dots = lax.dot_general(qi4, kic, (((1,), (1,)), ((), ())), preferred_element_type=F32)
        parts = []
        for u in range(reps):
            su = jnp.zeros((tq, LANES), F32)
            for h in range(IDX_HEADS):
                su = su + wb_ref[h] * jnp.maximum(dots[h * tq:(h + 1) * tq, u * LANES:(u + 1) * LANES], 0.0)
            kpos = lane + (off + u * LANES)
            parts.append(_sortable(jnp.where(kpos <= t_b, su * IDX_SCALE, -jnp.inf)))
        keys_ref[:, pl.ds(off, cw)] = _lane_tile(parts[0], 1) if reps == 1 else jnp.concatenate(parts, axis=1)
        return 0

    lax.fori_loop(0, nch, score_body, 0)

    tau, cut = _select_threshold(keys_ref, nch, cw, tq, k_top, idx_bits)
    tau_b = jnp.broadcast_to(tau, (tq, LANES))
    cut_b = jnp.broadcast_to(cut, (tq, LANES))

    q = q_ref[0] * (HEAD_DIM ** -0.5 * LOG2_E)
    qs = []
    for n in range(KV_HEADS):
        qs.append(jnp.concatenate([q[:, (2 * n) * HEAD_DIM:(2 * n + 1) * HEAD_DIM],
                                   q[:, (2 * n + 1) * HEAD_DIM:(2 * n + 2) * HEAD_DIM]], axis=0).astype(BF16))
    m_ref[...] = jnp.full(m_ref.shape, NEG_BIG, F32)
    acc_ref[...] = jnp.zeros(acc_ref.shape, F32)

    def attn_body(c, _):
        off = pl.multiple_of(c * cw, cw)
        kc = keys_ref[:, pl.ds(off, cw)]
        bias_parts = []
        for u in range(reps):
            ku = kc[:, u * LANES:(u + 1) * LANES]
            kpos = lane + (off + u * LANES)
            tie_ok = jnp.where(ku == tau_b, jnp.where(kpos <= cut_b, 0.0, NEG_BIG), NEG_BIG)
            sel = jnp.where(ku > tau_b, 0.0, tie_ok)
            bias_parts.append(jnp.where(kpos <= t_b, sel, NEG_BIG))
        bias = bias_parts[0] if reps == 1 else jnp.concatenate(bias_parts, axis=1)
        bias2 = jnp.concatenate([bias, bias], axis=0)
        ss = []
        for n in range(KV_HEADS):
            kn = k_ref[0, pl.ds(off, cw), n * HEAD_DIM:(n + 1) * HEAD_DIM]
            ss.append(lax.dot_general(qs[n], kn, (((1,), (1,)), ((), ())), preferred_element_type=F32) + bias2)
        ps, alphas = [], []
        for n in range(KV_HEADS):
            m_prev = m_ref[n]
            m_new = jnp.maximum(m_prev, jnp.max(ss[n], axis=1, keepdims=True))
            ps.append(jnp.exp2(ss[n] - _lane_tile(m_new, reps)).astype(BF16))
            alphas.append(jnp.exp2(m_prev - m_new))
            m_ref[n] = m_new
        for n in range(KV_HEADS):
            vn = v_ref[0, pl.ds(off, cw), n * LANES:(n + 1) * LANES]
            acc_ref[n] = alphas[n] * acc_ref[n] + jnp.dot(ps[n], vn, preferred_element_type=F32)
        return 0

    lax.fori_loop(0, nch, attn_body, 0)

    outs = []
    for n in range(KV_HEADS):
        acc = acc_ref[n]
        o = acc[:, :HEAD_DIM] / acc[:, HEAD_DIM:]
        outs += [o[:tq], o[tq:]]
    o_ref[0] = jnp.concatenate(outs, axis=1)


def dsa_prompt_attn(q, qi, kiwi, k_bf, v_aug, ki_bf, *, cw=512):
    B, S, _ = q.shape
    cw = min(cw, S)
    assert S % cw == 0 and cw % LANES == 0 and S % Q_BLOCK == 0
    k_top = min(TOPK_MAX, S // 4)
    blk = lambda w: pl.BlockSpec((1, Q_BLOCK, w), lambda b, j: (b, j, 0))
    res = lambda w: pl.BlockSpec((1, S, w), lambda b, j: (b, 0, 0), pipeline_mode=pl.Buffered(1))
    return pl.pallas_call(
        functools.partial(_dsa_prompt_kernel, k_top=k_top, cw=cw, idx_bits=max(1, (S - 1).bit_length())),
        grid=(B, S // Q_BLOCK),
        in_specs=[blk(ATT_W), blk(IDX_HEADS * IDX_DIM), blk(LANES), res(KV_W), res(2 * KV_W), res(IDX_DIM)],
        out_specs=blk(ATT_W),
        out_shape=jax.ShapeDtypeStruct((B, S, ATT_W), F32),
        scratch_shapes=[pltpu.VMEM((Q_BLOCK, S), jnp.int32),
                        pltpu.VMEM((IDX_HEADS, Q_BLOCK, LANES), F32),
                        pltpu.VMEM((KV_HEADS, 2 * Q_BLOCK, LANES), F32),
                        pltpu.VMEM((KV_HEADS, 2 * Q_BLOCK, LANES), F32)],
        compiler_params=_cparams("parallel", "arbitrary"),
        name="dsa_prompt",
    )(q, qi, kiwi, k_bf, v_aug, ki_bf)


def _dsa_sample_kernel(pt_ref, q_ref, qi_ref, kiwi_ref, kn_ref, vn_ref, ck_hbm, cv_hbm, cik_hbm, o_ref,
                       kbuf2, vbuf2, ibuf2, sems, keys_ref, s_ref, *, n_pages, k_top, cw, idx_bits):
    b = pl.program_id(0)
    n_b = pl.num_programs(0)
    t_new = q_ref.shape[1]
    past = n_pages * PAGE_SIZE
    ppc = cw // PAGE_SIZE
    nch = n_pages // ppc
    reps = cw // LANES
    slot = b % 2

    streams = ((cik_hbm, ibuf2), (ck_hbm, kbuf2), (cv_hbm, vbuf2))

    def page_copy(which, seq, sl, p):
        src, dst = streams[which]
        return pltpu.make_async_copy(src.at[pt_ref[seq, p]], dst.at[sl, p], sems.at[sl, which])

    def start_pages(seq, sl):
        def body(p, _):
            for which in range(len(streams)):
                page_copy(which, seq, sl, p).start()
            return 0
        lax.fori_loop(0, n_pages, body, 0)

    def wait_pages(which):
        def body(p, _):
            page_copy(which, b, slot, p).wait()
            return 0
        lax.fori_loop(0, n_pages, body, 0)

    @pl.when(b == 0)
    def _():
        start_pages(b, slot)

    @pl.when(b + 1 < n_b)
    def _():
        start_pages(b + 1, 1 - slot)

    ibuf = ibuf2.at[slot]
    kbuf = kbuf2.at[slot]
    vbuf = vbuf2.at[slot]

    lane = lax.broadcasted_iota(jnp.int32, (t_new, LANES), 1)
    t_b = lax.broadcasted_iota(jnp.int32, (t_new, LANES), 0)
    qi = qi_ref[0]
    qi4 = jnp.concatenate([qi[:, h * IDX_DIM:(h + 1) * IDX_DIM] for h in range(IDX_HEADS)], axis=0).astype(BF16)
    kiwi = kiwi_ref[0]
    wi = kiwi[:, IDX_DIM:IDX_DIM + IDX_HEADS]
    wb = [jnp.broadcast_to(wi[:, h:h + 1], (t_new, LANES)) for h in range(IDX_HEADS)]

    def combine(dots_u):
        su = jnp.zeros((t_new, LANES), F32)
        for h in range(IDX_HEADS):
            su = su + wb[h] * jnp.maximum(dots_u[h * t_new:(h + 1) * t_new], 0.0)
        return su * IDX_SCALE

    wait_pages(0)

    def score_body(c, _):
        kic = ibuf[pl.ds(c * ppc, ppc)].reshape(cw, IDX_DIM).astype(BF16)
        dots = lax.dot_general(qi4, kic, _NT, preferred_element_type=F32)
        parts = [_sortable(combine(dots[:, u * LANES:(u + 1) * LANES])) for u in range(reps)]
        keys_ref[:, pl.ds(pl.multiple_of(c * cw, cw), cw)] = jnp.concatenate(parts, axis=1)
        return 0

    lax.fori_loop(0, nch, score_body, 0)
    ki_new = jnp.concatenate([kiwi[:, :IDX_DIM], jnp.zeros((LANES - t_new, IDX_DIM), F32)], axis=0).astype(BF16)
    dots_new = lax.dot_general(qi4, ki_new, _NT, preferred_element_type=F32)
    s_new = jnp.where(lane <= t_b, combine(dots_new), -jnp.inf)
    neg = jnp.full((t_new, LANES), -jnp.inf, F32)
    keys_ref[:, pl.ds(past, cw)] = _sortable(jnp.concatenate([s_new] + [neg] * (reps - 1), axis=1))

    tau, cut = _select_threshold(keys_ref, nch + 1, cw, t_new, k_top, idx_bits)
    tau_b = jnp.broadcast_to(tau, (t_new, LANES))
    cut_b = jnp.broadcast_to(cut, (t_new, LANES))

    q = q_ref[0] * (HEAD_DIM ** -0.5 * LOG2_E)
    qs = []
    for n in range(KV_HEADS):
        qs.append(jnp.concatenate([q[:, (2 * n) * HEAD_DIM:(2 * n + 1) * HEAD_DIM],
                                   q[:, (2 * n + 1) * HEAD_DIM:(2 * n + 2) * HEAD_DIM]], axis=0).astype(BF16))
    rows2 = 2 * t_new

    def bias_of(kc, off, width, causal_new):
        parts = []
        for u in range(width // LANES):
            ku = kc[:, u * LANES:(u + 1) * LANES]
            kpos = lane + (off + u * LANES)
            tie_ok = jnp.where(ku == tau_b, jnp.where(kpos <= cut_b, 0.0, NEG_BIG), NEG_BIG)
            sel = jnp.where(ku > tau_b, 0.0, tie_ok)
            if causal_new:
                sel = jnp.where(lane <= t_b, sel, NEG_BIG)
            parts.append(sel)
        bias = parts[0] if len(parts) == 1 else jnp.concatenate(parts, axis=1)
        return jnp.concatenate([bias, bias], axis=0)

    def logits(kc_bf, bias2, off, mx):
        width = bias2.shape[1]
        new_mx = []
        for n in range(KV_HEADS):
            s = lax.dot_general(qs[n], kc_bf[:, n * HEAD_DIM:(n + 1) * HEAD_DIM], _NT, preferred_element_type=F32) + bias2
            s_ref[n, :, pl.ds(off, width)] = s
            m = mx[n]
            for u in range(width // LANES):
                m = jnp.maximum(m, s[:, u * LANES:(u + 1) * LANES])
            new_mx.append(m)
        return tuple(new_mx)

    wait_pages(1)

    def logit_body(c, mx):
        off = pl.multiple_of(c * cw, cw)
        kc = kbuf[pl.ds(c * ppc, ppc)].reshape(cw, KV_W).astype(BF16)
        return logits(kc, bias_of(keys_ref[:, pl.ds(off, cw)], off, cw, False), off, mx)

    mx = lax.fori_loop(0, nch, logit_body, tuple(jnp.full((rows2, LANES), NEG_BIG, F32) for _ in range(KV_HEADS)))
    pad_rows = jnp.zeros((LANES - t_new, KV_W), F32)
    kn_pad = jnp.concatenate([kn_ref[0], pad_rows], axis=0).astype(BF16)
    vn_pad = jnp.concatenate([vn_ref[0], pad_rows], axis=0).astype(BF16)
    mx = logits(kn_pad, bias_of(keys_ref[:, pl.ds(past, LANES)], past, LANES, True), past, mx)
    m_row = [jnp.broadcast_to(jnp.max(m, axis=1, keepdims=True), (rows2, LANES)) for m in mx]

    def weighted(vc_bf, off, width, carry):
        accs, sums = carry
        new_accs, new_sums = [], []
        for n in range(KV_HEADS):
            p = jnp.exp2(s_ref[n, :, pl.ds(off, width)] - _lane_tile(m_row[n], width // LANES))
            l = sums[n]
            for u in range(width // LANES):
                l = l + p[:, u * LANES:(u + 1) * LANES]
            new_sums.append(l)
            new_accs.append(accs[n] + jnp.dot(p.astype(BF16), vc_bf[:, n * HEAD_DIM:(n + 1) * HEAD_DIM],
                                              preferred_element_type=F32))
        return tuple(new_accs), tuple(new_sums)

    wait_pages(2)

    def pv_body(c, carry):
        vc = vbuf[pl.ds(c * ppc, ppc)].reshape(cw, KV_W).astype(BF16)
        return weighted(vc, pl.multiple_of(c * cw, cw), cw, carry)

    zero = lambda w: tuple(jnp.zeros((rows2, w), F32) for _ in range(KV_HEADS))
    carry = lax.fori_loop(0, nch, pv_body, (zero(HEAD_DIM), zero(LANES)))
    accs, sums = weighted(vn_pad, past, LANES, carry)

    outs = []
    for n in range(KV_HEADS):
        o = accs[n] / jnp.sum(sums[n], axis=1, keepdims=True)
        outs += [o[:t_new], o[t_new:]]
    o_ref[0] = jnp.concatenate(outs, axis=1)


def dsa_sample_attn(q, qi, kiwi, k_new, v_new, cache_k, cache_v, cache_idx_k, page_table, *, cw=512):
    Bd, T, _ = q.shape
    n_pages = page_table.shape[1]
    past = n_pages * PAGE_SIZE
    cw = min(cw, past)
    assert cw % PAGE_SIZE == 0 and past % cw == 0 and T <= LANES
    k_top = min(TOPK_MAX, (past + T) // 4)
    n_phys = cache_k.shape[0]
    blk = lambda w: pl.BlockSpec((1, T, w), lambda b, pt: (b, 0, 0))
    hbm = pl.BlockSpec(memory_space=pl.ANY)
    gs = pltpu.PrefetchScalarGridSpec(
        num_scalar_prefetch=1,
        grid=(Bd,),
        in_specs=[blk(ATT_W), blk(IDX_HEADS * IDX_DIM), blk(LANES), blk(KV_W), blk(KV_W), hbm, hbm, hbm],
        out_specs=blk(ATT_W),
        scratch_shapes=[pltpu.VMEM((2, n_pages, PAGE_SIZE, KV_W), F32),
                        pltpu.VMEM((2, n_pages, PAGE_SIZE, KV_W), F32),
                        pltpu.VMEM((2, n_pages, PAGE_SIZE, IDX_DIM), F32),
                        pltpu.SemaphoreType.DMA((2, 3)),
                        pltpu.VMEM((T, past + cw), jnp.int32),
                        pltpu.VMEM((KV_HEADS, 2 * T, past + LANES), F32)])
    return pl.pallas_call(
        functools.partial(_dsa_sample_kernel, n_pages=n_pages, k_top=k_top, cw=cw,
                          idx_bits=max(1, (past + cw - 1).bit_length())),
        grid_spec=gs,
        out_shape=jax.ShapeDtypeStruct((Bd, T, ATT_W), F32),
        compiler_params=_cparams("arbitrary"),
        name="dsa_sample",
    )(page_table, q, qi, kiwi, k_new, v_new,
      cache_k.reshape(n_phys, PAGE_SIZE, KV_W), cache_v.reshape(n_phys, PAGE_SIZE, KV_W), cache_idx_k)


def _cross_attn_kernel(x_ref, g_ref, wq_ref, mk_ref, mv_ref, wo_ref, o_ref):
    x = x_ref[0]
    q = jnp.dot(_rms(x, g_ref[...]).astype(BF16), wq_ref[...], preferred_element_type=F32) * (MEM_HD ** -0.5)
    q = q.astype(BF16)
    mk = mk_ref[0].astype(BF16)
    mv = mv_ref[0].astype(BF16)
    outs = []
    for h in range(MEM_HEADS):
        sl = slice(h * MEM_HD, (h + 1) * MEM_HD)
        s = lax.dot_general(q[:, sl], mk[:, sl], _NT, preferred_element_type=F32)
        p = jnp.exp(s - jnp.max(s, axis=1, keepdims=True))
        p = p / jnp.sum(p, axis=1, keepdims=True)
        outs.append(jnp.dot(p.astype(BF16), mv[:, sl], preferred_element_type=F32))
    o = jnp.concatenate(outs, axis=1).astype(BF16)
    o_ref[0] = x + jnp.dot(o, wo_ref[...], preferred_element_type=F32)


def cross_attn_mem(x, norm_g, wq_bf, mk, mv, wo_bf, *, tm=512):
    B, T, _ = x.shape
    tm = min(tm, T)
    assert T % tm == 0
    M = mk.shape[1]
    full = lambda a: pl.BlockSpec(a.shape, lambda b, i: (0,) * a.ndim)
    mem = pl.BlockSpec((1, M, MEM_W), lambda b, i: (b, 0, 0))
    return pl.pallas_call(
        _cross_attn_kernel,
        grid=(B, T // tm),
        in_specs=[pl.BlockSpec((1, tm, D_MODEL), lambda b, i: (b, i, 0)), full(norm_g), full(wq_bf), mem, mem, full(wo_bf)],
        out_specs=pl.BlockSpec((1, tm, D_MODEL), lambda b, i: (b, i, 0)),
        out_shape=jax.ShapeDtypeStruct(x.shape, F32),
        compiler_params=_cparams("parallel", "arbitrary"),
        name="cross_attn",
    )(x, norm_g, wq_bf, mk, mv, wo_bf)


def _norm_proj_kernel(x_ref, g_ref, w_ref, o_ref):
    o_ref[...] = jnp.dot(_rms(x_ref[...], g_ref[...]).astype(BF16), w_ref[...], preferred_element_type=F32)


def norm_proj(x2d, norm_g, w_bf, *, tm=256):
    n = x2d.shape[0]
    tm = min(tm, n)
    assert n % tm == 0
    return pl.pallas_call(
        _norm_proj_kernel,
        grid=(n // tm,),
        in_specs=[pl.BlockSpec((tm, x2d.shape[1]), lambda i: (i, 0)), pl.BlockSpec(norm_g.shape, lambda i: (0, 0)),
                  pl.BlockSpec(w_bf.shape, lambda i: (0, 0))],
        out_specs=pl.BlockSpec((tm, w_bf.shape[1]), lambda i: (i, 0)),
        out_shape=jax.ShapeDtypeStruct((n, w_bf.shape[1]), F32),
        compiler_params=_cparams("parallel"),
        name="norm_proj",
    )(x2d, norm_g, w_bf)


PEER_HALF = PEER_DK // 2
PEER_SUB = 256
PEER_TE = 2 * PEER_SUB
PEER_CAND_CAP = tuple(PEER_TOPK // (a + 1) for a in range(PEER_TOPK))
PEER_CAND_ROWS = -(-sum(PEER_CAND_CAP) // 8) * 8


def _extract_top(work_ref, n_keys, tn, on_pick):
    sub = lax.broadcasted_iota(jnp.int32, (n_keys, tn), 0).astype(F32)
    for a in range(PEER_TOPK):
        w = work_ref[...]
        m = jnp.max(w, axis=0, keepdims=True)
        idx = jnp.min(jnp.where(w == m, sub, float(n_keys)), axis=0, keepdims=True)
        hit = sub == idx
        work_ref[...] = jnp.where(hit, -jnp.inf, w)
        on_pick(a, m, hit)


def _peer_kernel(x_ref, g_ref, gf_ref, wq_ref, k1_ref, k2_ref, u_ref, vt_ref, o_ref,
                 xn_ref, s_ref, rank_ref, vals_ref, work_ref, cand_ref, c_ref, l_ref, e2_ref, acc_ref, ht_ref, w_ref,
                 *, tn, final_norm):
    e = pl.program_id(1)
    n_e = pl.num_programs(1)

    @pl.when(e == 0)
    def _route():
        xn = _rms(x_ref[...], g_ref[...]).astype(BF16)
        xn_ref[...] = xn
        q = jnp.dot(xn, wq_ref[...], preferred_element_type=F32).astype(BF16)
        for h in range(PEER_HEADS):
            q1 = q[:, h * PEER_DK:h * PEER_DK + PEER_HALF]
            q2 = q[:, h * PEER_DK + PEER_HALF:(h + 1) * PEER_DK]
            nt = (((1,), (1,)), ((), ()))
            s_ref[h] = lax.dot_general(k1_ref[h], q1, nt, preferred_element_type=F32)
            s_ref[PEER_HEADS + h] = lax.dot_general(k2_ref[h], q2, nt, preferred_element_type=F32)

        def top_keys(hh, _):
            work_ref[...] = s_ref[hh]
            rank_ref[hh] = jnp.full((N_KEYS, tn), float(PEER_TOPK), F32)

            def pick(a, m, hit):
                vals_ref[hh, a:a + 1, :] = m
                rank_ref[hh] = jnp.where(hit, float(a), rank_ref[hh])
            _extract_top(work_ref, N_KEYS, tn, pick)
            return 0

        lax.fori_loop(0, 2 * PEER_HEADS, top_keys, 0)

        def gates(h, _):
            v1 = vals_ref[h]
            v2 = vals_ref[PEER_HEADS + h]
            rows = [v1[a:a + 1, :] + v2[0:cap, :] for a, cap in enumerate(PEER_CAND_CAP)]
            rows.append(jnp.full((PEER_CAND_ROWS - sum(PEER_CAND_CAP), tn), -jnp.inf, F32))
            cand_ref[...] = jnp.concatenate(rows, axis=0)
            _extract_top(cand_ref, PEER_CAND_ROWS, tn, lambda a, m, hit: None)
            picked = jnp.where(cand_ref[...] == -jnp.inf, 1.0, 0.0)
            e1 = jnp.exp(v1 - v1[0:1, :])
            e2 = jnp.exp(v2 - v2[0:1, :])
            z = jnp.zeros((1, tn), F32)
            r1 = rank_ref[h]
            lfull = jnp.zeros((N_KEYS, tn), F32)
            off = 0
            for a, cap in enumerate(PEER_CAND_CAP):
                sel = picked[off:off + cap, :]
                off += cap
                z = z + e1[a:a + 1, :] * jnp.sum(sel * e2[0:cap, :], axis=0, keepdims=True)
                lfull = lfull + jnp.where(r1 == float(a), jnp.sum(sel, axis=0, keepdims=True), 0.0)
            l_ref[h] = lfull
            c_ref[h] = jnp.where(r1 < float(PEER_TOPK), jnp.exp(s_ref[h] - v1[0:1, :]) / z, 0.0)
            e2_ref[h] = jnp.exp(s_ref[PEER_HEADS + h] - v2[0:1, :])
            return 0

        lax.fori_loop(0, PEER_HEADS, gates, 0)
        acc_ref[...] = jnp.zeros(acc_ref.shape, F32)

        ht_ref[...] = jnp.zeros(ht_ref.shape, F32)
        w_ref[...] = jnp.zeros(w_ref.shape, BF16)

    n_sub = 2 * (n_e - 1)
    keys_per_sub = PEER_SUB // N_KEYS

    def gated(ht, sub):
        live = jnp.where(jnp.logical_and(sub >= 0, sub < n_sub), 1.0, 0.0)
        base = jnp.clip(sub, 0, n_sub - 1) * keys_per_sub
        parts = []
        for ii in range(keys_per_sub):
            g = jnp.zeros((N_KEYS, tn), F32)
            for h in range(PEER_HEADS):
                cb = c_ref[h, pl.ds(base + ii, 1), :] * live
                lb = l_ref[h, pl.ds(base + ii, 1), :]
                g = g + cb * jnp.where(rank_ref[PEER_HEADS + h] < lb, e2_ref[h], 0.0)
            hi = ht[ii * N_KEYS:(ii + 1) * N_KEYS]
            act = 0.5 * hi * (1.0 + lax.erf(hi * (2.0 ** -0.5)))
            parts.append((g * act).astype(BF16))
        return parts[0] if len(parts) == 1 else jnp.concatenate(parts, axis=0)

    nt = (((1,), (1,)), ((), ()))
    w_old = w_ref[...]
    w_mid = gated(ht_ref[...], 2 * e - 1)
    acc_ref[...] += (jnp.dot(vt_ref[:, :PEER_SUB], w_old, preferred_element_type=F32)
                     + jnp.dot(vt_ref[:, PEER_SUB:], w_mid, preferred_element_type=F32))
    ht_new = lax.dot_general(u_ref[:PEER_SUB, :], xn_ref[...], nt, preferred_element_type=F32)
    w_ref[...] = gated(ht_new, 2 * e)
    ht_ref[...] = lax.dot_general(u_ref[PEER_SUB:, :], xn_ref[...], nt, preferred_element_type=F32)

    @pl.when(e == n_e - 1)
    def _finish():
        y = x_ref[...] + acc_ref[...].T
        if final_norm:
            y = _rms(y, gf_ref[...])
        o_ref[...] = y


def peer_ffn(x2d, norm_g, final_g, wq_bf, k1_bf, k2_bf, u_bf, vt_bf, *, final_norm, tn=512):
    n = x2d.shape[0]
    tn = min(tn, n)
    assert n % tn == 0 and tn % LANES == 0
    n_blk = u_bf.shape[0] // PEER_TE
    full = lambda a: pl.BlockSpec(a.shape, lambda t, e: (0,) * a.ndim)
    return pl.pallas_call(
        functools.partial(_peer_kernel, tn=tn, final_norm=final_norm),
        grid=(n // tn, n_blk + 1),
        in_specs=[pl.BlockSpec((tn, D_MODEL), lambda t, e: (t, 0)), full(norm_g), full(final_g), full(wq_bf),
                  full(k1_bf), full(k2_bf),
                  pl.BlockSpec((PEER_TE, D_MODEL), lambda t, e: (jnp.minimum(e, n_blk - 1), 0)),
                  pl.BlockSpec((D_MODEL, PEER_TE), lambda t, e: (0, jnp.maximum(e - 1, 0)))],
        out_specs=pl.BlockSpec((tn, D_MODEL), lambda t, e: (t, 0)),
        out_shape=jax.ShapeDtypeStruct((n, D_MODEL), F32),
        scratch_shapes=[pltpu.VMEM((tn, D_MODEL), BF16),
                        pltpu.VMEM((2 * PEER_HEADS, N_KEYS, tn), F32),
                        pltpu.VMEM((2 * PEER_HEADS, N_KEYS, tn), F32),
                        pltpu.VMEM((2 * PEER_HEADS, PEER_TOPK, tn), F32),
                        pltpu.VMEM((N_KEYS, tn), F32),
                        pltpu.VMEM((PEER_CAND_ROWS, tn), F32),
                        pltpu.VMEM((PEER_HEADS, N_KEYS, tn), F32),
                        pltpu.VMEM((PEER_HEADS, N_KEYS, tn), F32),
                        pltpu.VMEM((PEER_HEADS, N_KEYS, tn), F32),
                        pltpu.VMEM((D_MODEL, tn), F32),
                        pltpu.VMEM((PEER_SUB, tn), F32),
                        pltpu.VMEM((PEER_SUB, tn), BF16)],
        compiler_params=_cparams("parallel", "arbitrary"),
        name="peer_ffn",
    )(x2d, norm_g, final_g, wq_bf, k1_bf, k2_bf, u_bf, vt_bf)


RWKV_GROUP = 4
RWKV_GW = RWKV_GROUP * RWKV_N
RWKV_CHUNK = 64


def _split_bf16(x, terms):
    parts = []
    for _ in range(terms):
        p = x.astype(BF16)
        parts.append(p)
        x = x - p.astype(F32)
    return parts


def _dot3(a, b, dims=(((1,), (0,)), ((), ()))):
    ah, al = _split_bf16(a, 2)
    bh, bl = _split_bf16(b, 2)
    d = lambda x, y: lax.dot_general(x, y, dims, preferred_element_type=F32)
    return d(ah, bh) + d(al, bh) + d(ah, bl)


def _dot_exact_rhs(a, b_exact_bf16):
    return sum(jnp.dot(p, b_exact_bf16, preferred_element_type=F32) for p in _split_bf16(a, 3))


def _dot_exact_lhs(a_exact_bf16, b):
    return sum(jnp.dot(a_exact_bf16, p, preferred_element_type=F32) for p in _split_bf16(b, 3))


_NT = (((1,), (1,)), ((), ()))
_TN = (((0,), (0,)), ((), ()))


def _rwkv_kernel(feat_ref, prev_ref, s0t_ref, mu_ref, w0_ref, w2_ref, a0_ref, a2_ref, g2_ref, kk_ref, ka_ref, rk_ref,
                 lng_ref, lnb_ref, hsum_ref, tri_ref, o_ref, sfin_ref, state_ref, carry_ref, *, c_len, n_double):
    c = pl.program_id(1)
    n_c = pl.num_programs(1)
    n4 = RWKV_GROUP * c_len

    @pl.when(c == 0)
    def _init():
        carry_ref[0:1, :] = prev_ref[0]
        state_ref[...] = jnp.zeros(state_ref.shape, F32)
        for h in range(RWKV_HEADS):
            g, hh = divmod(h, RWKV_GROUP)
            state_ref[g, hh * RWKV_N:(hh + 1) * RWKV_N, hh * RWKV_N:(hh + 1) * RWKV_N] = s0t_ref[0, h]

    feat = feat_ref[0]
    row = lax.broadcasted_iota(jnp.int32, feat.shape, 0)
    shifted = jnp.where(row == 0, jnp.broadcast_to(carry_ref[0:1, :], feat.shape), pltpu.roll(feat, 1, 0))
    carry_ref[0:1, :] = feat[c_len - 1:c_len, :]
    mixed = feat + (shifted - feat) * mu_ref[...]
    r = mixed[:, 0:RWKV_W]
    k = mixed[:, RWKV_W:2 * RWKV_W]
    v = mixed[:, 2 * RWKV_W:3 * RWKV_W]
    o1 = 3 * RWKV_W
    wd = mixed[:, o1:o1 + LORA_W]
    ad = mixed[:, o1 + LORA_W:o1 + LORA_W + LORA_A]
    gd = mixed[:, o1 + LORA_W + LORA_A:]

    zw = -(w0_ref[...] + _dot3(jnp.tanh(wd), w2_ref[...]))
    log_w = -(jnp.maximum(zw, 0.0) + jnp.log1p(jnp.exp(-jnp.abs(zw)))) - 0.5
    dlog = -jnp.exp(log_w)
    a = jax.nn.sigmoid(a0_ref[...] + _dot3(ad, a2_ref[...]))
    gate = _dot3(jax.nn.sigmoid(gd), g2_ref[...])
    hsum = hsum_ref[...]
    kk = k * kk_ref[...]
    kk = kk / jnp.maximum(jnp.sqrt(_dot_exact_rhs(kk * kk, hsum)), 1e-12)
    k_eff = k * (1.0 + (a - 1.0) * ka_ref[...])

    cum = _dot_exact_lhs(tri_ref[...], dlog)
    g_in = jnp.exp(cum)
    g_inv = jnp.exp(-cum)
    kk_t = kk * jnp.exp(cum - dlog)
    b_h = kk * a * g_inv
    k_h = k_eff * g_inv
    r_t = r * g_in
    g_last = g_in[c_len - 1:c_len, :]

    rr = lax.broadcasted_iota(jnp.int32, (n4, n4), 0)
    cc = lax.broadcasted_iota(jnp.int32, (n4, n4), 1)
    eye = jnp.where(rr == cc, 1.0, 0.0)
    lane_g = lax.broadcasted_iota(jnp.int32, (c_len, RWKV_GW), 1) // RWKV_N
    rr_g = lax.broadcasted_iota(jnp.int32, (RWKV_GW, RWKV_GW), 0)
    cc_g = lax.broadcasted_iota(jnp.int32, (RWKV_GW, RWKV_GW), 1)

    ys = []
    for g in range(RWKV_HEADS // RWKV_GROUP):
        sl = slice(g * RWKV_GW, (g + 1) * RWKV_GW)

        def stack(x):
            xg = x[:, sl]
            return jnp.concatenate([jnp.where(lane_g == h, xg, 0.0) for h in range(RWKV_GROUP)], axis=0)

        kks, rs, bs, ks, vs = stack(kk_t), stack(r_t), stack(b_h), stack(k_h), stack(v)
        l_b = jnp.where(rr > cc, _dot3(kks, bs, _NT), 0.0)
        l_k = jnp.where(rr > cc, _dot3(kks, ks, _NT), 0.0)
        m_b = jnp.where(rr >= cc, _dot3(rs, bs, _NT), 0.0)
        m_k = jnp.where(rr >= cc, _dot3(rs, ks, _NT), 0.0)
        npow = -l_b
        t_inv = eye + npow
        for _ in range(n_double):
            npow = _dot3(npow, npow)
            t_inv = t_inv + _dot3(t_inv, npow)
        w1 = -_dot3(t_inv, kks)
        u1 = -_dot3(t_inv, _dot3(l_k, vs))
        gl = g_last[:, sl]
        bsg = bs * gl
        ksg = ks * gl
        g_mat = jnp.where(rr_g == cc_g, jnp.broadcast_to(gl, (RWKV_GW, RWKV_GW)), 0.0) + _dot3(bsg, w1, _TN)
        h_mat = _dot3(bsg, u1, _TN) + _dot3(ksg, vs, _TN)
        r_y = rs + _dot3(m_b, w1)
        y_0 = _dot3(m_b, u1) + _dot3(m_k, vs)
        s_t = state_ref[g]
        y_st = _dot3(r_y, s_t) + y_0
        state_ref[g] = _dot3(g_mat, s_t) + h_mat
        y = y_st[0:c_len]
        for h in range(1, RWKV_GROUP):
            y = y + y_st[h * c_len:(h + 1) * c_len]
        ys.append(y)
    y = jnp.concatenate(ys, axis=1)

    inv_n = 1.0 / RWKV_N
    mean = _dot_exact_rhs(y, hsum) * inv_n
    yc = y - mean
    var = _dot_exact_rhs(yc * yc, hsum) * inv_n
    yn = yc * lax.rsqrt(var + LN_X_EPS) * lng_ref[...] + lnb_ref[...]
    bonus = _dot_exact_rhs(r * k_eff * rk_ref[...], hsum) * v
    o_ref[0] = (yn + bonus) * gate

    @pl.when(c == n_c - 1)
    def _fin():
        for h in range(RWKV_HEADS):
            g, hh = divmod(h, RWKV_GROUP)
            sfin_ref[0, h] = state_ref[g, hh * RWKV_N:(hh + 1) * RWKV_N, hh * RWKV_N:(hh + 1) * RWKV_N]


def rwkv_mix(feat, prev, s0, lw, *, c_len):
    B, T, _ = feat.shape
    assert T % c_len == 0 and c_len % 8 == 0
    n_double = max(0, (c_len - 1).bit_length() - 1)
    hsum = (jnp.arange(RWKV_W)[:, None] // RWKV_N == jnp.arange(RWKV_W)[None, :] // RWKV_N).astype(BF16)
    tri = (jnp.arange(c_len)[:, None] >= jnp.arange(c_len)[None, :]).astype(BF16)
    row = lambda a: a.reshape(1, -1).astype(F32)
    params = [row(lw['rwkv_mu']), row(lw['rwkv_w0']), lw['rwkv_w2'], row(lw['rwkv_a0']), lw['rwkv_a2'], lw['rwkv_g2'],
              row(lw['rwkv_k_k']), row(lw['rwkv_k_a']), row(lw['rwkv_r_k']), row(lw['rwkv_ln_g']), row(lw['rwkv_ln_b']),
              hsum, tri]
    full = lambda a: pl.BlockSpec(a.shape, lambda b, c: (0,) * a.ndim)
    st_spec = pl.BlockSpec((1, RWKV_HEADS, RWKV_N, RWKV_N), lambda b, c: (b, 0, 0, 0))
    out, s_fin_t = pl.pallas_call(
        functools.partial(_rwkv_kernel, c_len=c_len, n_double=n_double),
        grid=(B, T // c_len),
        in_specs=[pl.BlockSpec((1, c_len, SHIFT_W), lambda b, c: (b, c, 0)),
                  pl.BlockSpec((1, 1, SHIFT_W), lambda b, c: (b, 0, 0)), st_spec] + [full(p) for p in params],
        out_specs=[pl.BlockSpec((1, c_len, RWKV_W), lambda b, c: (b, c, 0)), st_spec],
        out_shape=[jax.ShapeDtypeStruct((B, T, RWKV_W), F32),
                   jax.ShapeDtypeStruct((B, RWKV_HEADS, RWKV_N, RWKV_N), F32)],
        scratch_shapes=[pltpu.VMEM((RWKV_HEADS // RWKV_GROUP, RWKV_GW, RWKV_GW), F32),
                        pltpu.VMEM((8, SHIFT_W), F32)],
        compiler_params=_cparams("parallel", "arbitrary"),
        name="rwkv_mix",
    )(feat, prev[:, None, :], jnp.swapaxes(s0, -1, -2), *params)
    return out, jnp.swapaxes(s_fin_t, -1, -2)


def _rope_inv_tiled():
    half = HEAD_DIM // 2
    inv = ROPE_THETA ** (-jnp.arange(half, dtype=jnp.float32) / half)
    return jnp.tile(inv, LANES // half)[None, :]


def trunk_layer(x, period, offset, lw, paged, shift_prev, wkv0, mk, mv, final_g, is_last):
    B, T, _ = x.shape
    n = B * T
    x2d = x.reshape(n, D_MODEL)
    proj = in_proj(x2d, lw['norm_mix'], lw['w_in_packed'], _rope_inv_tiled(), period=period, offset=offset,
                   attn_operands=paged is None)
    q, k, v, qi, kiwi, feat, ga, gb = proj[:8]
    b3 = lambda a: a.reshape(B, T, a.shape[-1])
    ki = kiwi[:, :IDX_DIM].reshape(B, T, IDX_DIM)
    feat = b3(feat)
    if paged is None:
        k_bf, ki_bf, v_aug = proj[8:]
        o_a = dsa_prompt_attn(b3(q), b3(qi), b3(kiwi), b3(k_bf), b3(v_aug), b3(ki_bf))
    else:
        o_a = dsa_sample_attn(b3(q), b3(qi), b3(kiwi), b3(k), b3(v), *paged)
    o_b, wkv_fin = rwkv_mix(feat, shift_prev, wkv0, lw, c_len=min(T, RWKV_CHUNK))
    shift_last = feat[:, -1]
    x2d = mix_out(x2d, o_a.reshape(n, ATT_W), o_b.reshape(n, RWKV_W), ga, gb, lw['proj_a'], lw['proj_b'], lw['w_out'])
    x = cross_attn_mem(x2d.reshape(B, T, D_MODEL), lw['norm_ca'], lw['w_cq'], mk, mv, lw['w_co'])
    x2d = peer_ffn(x.reshape(n, D_MODEL), lw['norm_ffn'], final_g, lw['peer_wq'], lw['peer_k1'],
                   lw['peer_k2'], lw['peer_u'], lw['peer_vt'], final_norm=is_last)
    new_k = k.reshape(B, T, KV_HEADS, HEAD_DIM)
    new_v = v.reshape(B, T, KV_HEADS, HEAD_DIM)
    return x2d.reshape(B, T, D_MODEL), (new_k, new_v, ki, shift_last, wkv_fin)


def kernel(x_prompt, x_sample, cache_k, cache_v, cache_idx_k, state_shift, state_wkv, cache_mem_k, cache_mem_v, page_table, mem_prompt, norm_mix, w_in, rwkv_mu, rwkv_w0, rwkv_w2, rwkv_a0, rwkv_a2, rwkv_g2, rwkv_k_k, rwkv_k_a, rwkv_r_k, rwkv_ln_g, rwkv_ln_b, proj_a, proj_b, w_out, norm_ca, norm_mem, w_cq, w_mk, w_mv, w_co, norm_ffn, peer_wq, peer_k1, peer_k2, peer_u, peer_v, norm_final):
    B, S, _ = x_prompt.shape
    Bd, T, _ = x_sample.shape
    depth = w_in.shape[0]
    past = page_table.shape[1] * PAGE_SIZE
    xp, xs = x_prompt, x_sample
    st_p_all, st_s_all, mem_all = [], [], []
    bf = lambda a: a.astype(BF16)
    row = lambda a: a.reshape(1, -1)
    final_g = row(norm_final)
    M = mem_prompt.shape[1]
    for l in range(depth):
        lw = {
            'norm_mix': row(norm_mix[l]), 'w_in_packed': _pack_w_in(w_in[l]), 'rwkv_mu': rwkv_mu[l], 'rwkv_w0': rwkv_w0[l],
            'rwkv_w2': rwkv_w2[l], 'rwkv_a0': rwkv_a0[l], 'rwkv_a2': rwkv_a2[l], 'rwkv_g2': rwkv_g2[l],
            'rwkv_k_k': rwkv_k_k[l], 'rwkv_k_a': rwkv_k_a[l], 'rwkv_r_k': rwkv_r_k[l], 'rwkv_ln_g': rwkv_ln_g[l],
            'rwkv_ln_b': rwkv_ln_b[l], 'proj_a': bf(proj_a[l]), 'proj_b': bf(proj_b[l]), 'w_out': bf(w_out[l]),
            'norm_ca': row(norm_ca[l]), 'w_cq': bf(w_cq[l]), 'w_co': bf(w_co[l]), 'norm_ffn': row(norm_ffn[l]),
            'peer_wq': bf(peer_wq[l]), 'peer_k1': bf(peer_k1[l]), 'peer_k2': bf(peer_k2[l]),
            'peer_u': bf(peer_u[l]), 'peer_vt': bf(peer_v[l].T),
        }
        is_last = l == depth - 1
        mem_kv = norm_proj(mem_prompt.reshape(B * M, D_MODEL), row(norm_mem[l]),
                           bf(jnp.concatenate([w_mk[l], w_mv[l]], axis=1)))
        mk_p = mem_kv[:, :MEM_W].reshape(B, M, MEM_W)
        mv_p = mem_kv[:, MEM_W:].reshape(B, M, MEM_W)
        xp, st_p = trunk_layer(xp, S, 0, lw, None,
                               jnp.zeros((B, SHIFT_W), F32), jnp.zeros((B, RWKV_HEADS, RWKV_N, RWKV_N), F32),
                               mk_p, mv_p, final_g, is_last)
        paged = (cache_k[l], cache_v[l], cache_idx_k[l], page_table)
        xs, st_s = trunk_layer(xs, T, past, lw, paged, state_shift[l], state_wkv[l],
                               cache_mem_k[l].reshape(Bd, M, MEM_W), cache_mem_v[l].reshape(Bd, M, MEM_W), final_g, is_last)
        st_p_all.append(st_p)
        st_s_all.append(st_s)
        mem_all.append((mk_p.reshape(B, M, MEM_HEADS, MEM_HD), mv_p.reshape(B, M, MEM_HEADS, MEM_HD)))
    y_prompt, y_sample = xp, xs
    stack = lambda lst, i: jnp.stack([s[i] for s in lst])
    return (y_prompt, y_sample,
            stack(st_p_all, 0), stack(st_p_all, 1), stack(st_p_all, 2), stack(st_p_all, 3), stack(st_p_all, 4),
            stack(mem_all, 0), stack(mem_all, 1),
            stack(st_s_all, 0), stack(st_s_all, 1), stack(st_s_all, 2), stack(st_s_all, 3), stack(st_s_all, 4))
```

```python
import functools
import math

import jax
import jax.numpy as jnp
import numpy as np
from jax import lax
from jax.experimental import pallas as pl
from jax.experimental.pallas import tpu as pltpu

D_MODEL = 1024
PAGE_SIZE = 128
ATT_HEADS = 8
KV_HEADS = 4
HEAD_DIM = 64
ATT_W = ATT_HEADS * HEAD_DIM
KV_W = KV_HEADS * HEAD_DIM
IDX_HEADS = 4
IDX_DIM = 64
TOPK_MAX = 256
Q_BLOCK = 128
ROPE_THETA = 10000.0
RWKV_HEADS = 8
RWKV_N = 64
RWKV_W = RWKV_HEADS * RWKV_N
LORA_W = 64
LORA_A = 64
LORA_G = 128
LN_X_EPS = 64e-5
SHIFT_W = 3 * RWKV_W + LORA_W + LORA_A + LORA_G
MEM_HEADS = 4
MEM_HD = 128
MEM_W = MEM_HEADS * MEM_HD
PEER_HEADS = 8
PEER_DK = 128
N_KEYS = 128
PEER_TOPK = 16
PEER_BLOCK = 128
NORM_EPS = 1e-6
SPLITS = (ATT_W, KV_W, KV_W, IDX_HEADS * IDX_DIM, IDX_DIM, IDX_HEADS, SHIFT_W, 2 * D_MODEL)

LANES = 128
VMEM_LIMIT = 56 * 1024 * 1024

F32 = jnp.float32
BF16 = jnp.bfloat16


def _cparams(*sem):
    return pltpu.CompilerParams(dimension_semantics=sem, vmem_limit_bytes=VMEM_LIMIT)


def _rms(x, g):
    return x * lax.rsqrt(jnp.mean(x * x, axis=-1, keepdims=True) + NORM_EPS) * g


_SEG_Q = (0, 512)
_SEG_K = (512, 768)
_SEG_V = (768, 1024)
_SEG_QI = (1024, 1280)
_SEG_KIWI = (1280, 1408)
_SEG_FEAT = (1408, 3200)
_SEG_GATE = (3200, 5248)
_SEG_VAUG = (5248, 5760)
_PACKED_COLS = 5760


def _pack_w_in(w_in):
    pts = np.cumsum(SPLITS)[:-1].tolist()
    wq, wk, wv, wqi, wki, wwi, wfeat, wgate = jnp.split(w_in, pts, axis=-1)
    d = w_in.shape[0]
    pad = jnp.zeros((d, LANES - IDX_DIM - IDX_HEADS), w_in.dtype)
    wv_aug = jnp.concatenate([wv.reshape(d, KV_HEADS, HEAD_DIM), jnp.zeros((d, KV_HEADS, HEAD_DIM), w_in.dtype)],
                             axis=-1).reshape(d, 2 * KV_W)
    return jnp.concatenate([wq, wk, wv, wqi, wki, wwi, pad, wfeat, wgate, wv_aug], axis=-1).astype(BF16)


def _rope_lanes(y, cos, sin_signed, first_half):
    outs = []
    for c in range(y.shape[1] // LANES):
        xc = y[:, c * LANES:(c + 1) * LANES]
        partner = jnp.where(first_half, pltpu.roll(xc, 96, 1), pltpu.roll(xc, 32, 1))
        outs.append(xc * cos + partner * sin_signed)
    return outs[0] if len(outs) == 1 else jnp.concatenate(outs, axis=1)


def _in_proj_kernel(x_ref, g_ref, w_ref, inv_ref, q_ref, k_ref, v_ref, qi_ref, kiwi_ref, feat_ref, ga_ref, gb_ref,
                    *bf_refs, tm, period, offset):
    x = x_ref[...]
    h = _rms(x, g_ref[...]).astype(BF16)
    row = pl.program_id(0) * tm + lax.broadcasted_iota(jnp.int32, (tm, LANES), 0)
    pos = (row % period + offset).astype(F32)
    ang = pos * inv_ref[...]
    cos = jnp.cos(ang)
    sin = jnp.sin(ang)
    lane = lax.broadcasted_iota(jnp.int32, (tm, LANES), 1)
    first_half = (lane % HEAD_DIM) < (HEAD_DIM // 2)
    sin_signed = jnp.where(first_half, -sin, sin)

    def seg(s):
        return jnp.dot(h, w_ref[:, s[0]:s[1]], preferred_element_type=F32)

    q_ref[...] = _rope_lanes(seg(_SEG_Q), cos, sin_signed, first_half)
    k = _rope_lanes(seg(_SEG_K), cos, sin_signed, first_half)
    k_ref[...] = k
    v_ref[...] = seg(_SEG_V)
    qi_ref[...] = _rope_lanes(seg(_SEG_QI), cos, sin_signed, first_half)
    kiwi = seg(_SEG_KIWI)
    kiwi = jnp.where(lane < IDX_DIM, _rope_lanes(kiwi, cos, sin_signed, first_half), kiwi)
    kiwi_ref[...] = kiwi
    feat_ref[...] = seg(_SEG_FEAT)
    gates = jax.nn.sigmoid(seg(_SEG_GATE))
    ga_ref[...] = gates[:, :D_MODEL]
    gb_ref[...] = gates[:, D_MODEL:]
    if bf_refs:
        kbf_ref, kibf_ref, vaug_ref = bf_refs
        kbf_ref[...] = k.astype(BF16)
        kibf_ref[...] = kiwi[:, :IDX_DIM].astype(BF16)
        ones_half = jnp.where(lane >= HEAD_DIM, 1.0, 0.0)
        vaug_ref[...] = (seg(_SEG_VAUG) + jnp.concatenate([ones_half] * (2 * KV_W // LANES), axis=1)).astype(BF16)


def in_proj(x2d, norm_g, w_packed, inv_tiled, *, period, offset, attn_operands=False, tm=256):
    n = x2d.shape[0]
    assert n % tm == 0
    outs = [(w, F32) for w in (ATT_W, KV_W, KV_W, IDX_HEADS * IDX_DIM, LANES, SHIFT_W, D_MODEL, D_MODEL)]
    if attn_operands:
        outs += [(KV_W, BF16), (IDX_DIM, BF16), (2 * KV_W, BF16)]
    row = lambda w: pl.BlockSpec((tm, w), lambda i: (i, 0))
    full = lambda a: pl.BlockSpec(a.shape, lambda i: (0,) * a.ndim)
    return pl.pallas_call(
        functools.partial(_in_proj_kernel, tm=tm, period=period, offset=offset),
        grid=(n // tm,),
        in_specs=[row(D_MODEL), full(norm_g), full(w_packed), full(inv_tiled)],
        out_specs=[row(w) for w, _ in outs],
        out_shape=[jax.ShapeDtypeStruct((n, w), dt) for w, dt in outs],
        compiler_params=_cparams("parallel"),
        name="in_proj",
    )(x2d, norm_g, w_packed, inv_tiled)


def _mix_out_kernel(x_ref, oa_ref, ob_ref, ga_ref, gb_ref, pa_ref, pb_ref, wo_ref, o_ref):
    ya = jnp.dot(oa_ref[...].astype(BF16), pa_ref[...], preferred_element_type=F32)
    yb = jnp.dot(ob_ref[...].astype(BF16), pb_ref[...], preferred_element_type=F32)
    m = ga_ref[...] * ya + gb_ref[...] * yb
    o_ref[...] = x_ref[...] + jnp.dot(m.astype(BF16), wo_ref[...], preferred_element_type=F32)


def mix_out(x2d, oa, ob, ga, gb, pa, pb, wo, *, tm=256):
    n = x2d.shape[0]
    row = lambda w: pl.BlockSpec((tm, w), lambda i: (i, 0))
    full = lambda a: pl.BlockSpec(a.shape, lambda i: (0,) * a.ndim)
    return pl.pallas_call(
        _mix_out_kernel,
        grid=(n // tm,),
        in_specs=[row(D_MODEL), row(ATT_W), row(RWKV_W), row(D_MODEL), row(D_MODEL), full(pa), full(pb), full(wo)],
        out_specs=row(D_MODEL),
        out_shape=jax.ShapeDtypeStruct((n, D_MODEL), F32),
        compiler_params=_cparams("parallel"),
        name="mix_out",
    )(x2d, oa, ob, ga, gb, pa, pb, wo)


LOG2_E = math.log2(math.e)
INT_MIN = -(2 ** 31)
NEG_BIG = -1e30
IDX_SCALE = IDX_DIM ** -0.5 * IDX_HEADS ** -0.5


def _sortable(s):
    b = lax.bitcast_convert_type(s, jnp.int32)
    return b ^ ((b >> 31) & 0x7FFFFFFF)


def _lane_tile(x, reps):
    return x if reps == 1 else jnp.concatenate([x] * reps, axis=1)


def _select_threshold(keys_ref, nch, cw, rows, k_top):
    reps = cw // LANES

    def count_ge(thr_col):
        thr = jnp.broadcast_to(thr_col, (rows, LANES))

        def body(c, acc):
            kc = keys_ref[:, pl.ds(pl.multiple_of(c * cw, cw), cw)]
            for u in range(reps):
                acc = acc + jnp.where(kc[:, u * LANES:(u + 1) * LANES] >= thr, 1.0, 0.0)
            return acc
        acc = lax.fori_loop(0, nch, body, jnp.zeros((rows, LANES), F32))
        return jnp.sum(acc, axis=1, keepdims=True)

    def value_bit(i, t_u):
        cand = t_u | jnp.left_shift(jnp.int32(1), 31 - i)
        return jnp.where(count_ge(cand ^ INT_MIN) >= k_top, cand, t_u)

    t_u = lax.fori_loop(0, 32, value_bit, jnp.zeros((rows, 1), jnp.int32))
    tau = t_u ^ INT_MIN
    return tau, k_top - count_ge(tau + 1)


def _prefix_ones(n):
    return (jnp.arange(n)[:, None] <= jnp.arange(n)[None, :]).astype(BF16)


def _selection_bias(kc, tau_b, need_b, seen, tri):
    reps = kc.shape[1] // LANES
    eq = kc == _lane_tile(tau_b, reps)
    eq01 = jnp.where(eq, 1.0, 0.0)
    rank = jnp.dot(eq01.astype(BF16), tri, preferred_element_type=F32) + _lane_tile(seen, reps)
    tie_ok = jnp.where(eq, jnp.where(rank <= _lane_tile(need_b, reps), 0.0, NEG_BIG), NEG_BIG)
    bias = jnp.where(kc > _lane_tile(tau_b, reps), 0.0, tie_ok)
    return bias, seen + jnp.sum(eq01, axis=1, keepdims=True)


def _dsa_prompt_kernel(q_ref, qi_ref, kiwi_ref, k_ref, v_ref, ki_ref, tri_ref, o_ref,
                       keys_ref, wb_ref, m_ref, acc_ref, *, k_top, cw):
    tq = Q_BLOCK
    j = pl.program_id(1)
    nch = (j * tq + tq + cw - 1) // cw
    reps = cw // LANES
    t_b = j * tq + lax.broadcasted_iota(jnp.int32, (tq, LANES), 0)
    lane = lax.broadcasted_iota(jnp.int32, (tq, LANES), 1)

    qi = qi_ref[0]
    qi4 = jnp.concatenate([qi[:, h * IDX_DIM:(h + 1) * IDX_DIM] for h in range(IDX_HEADS)], axis=0).astype(BF16)
    wi = kiwi_ref[0][:, IDX_DIM:IDX_DIM + IDX_HEADS]
    for h in range(IDX_HEADS):
        wb_ref[h] = jnp.broadcast_to(wi[:, h:h + 1], (tq, LANES))

    def score_body(c, _):
        off = pl.multiple_of(c * cw, cw)
        kic = ki_ref[0, pl.ds(off, cw), :]
        dots = lax.dot_general(qi4, kic, (((1,), (1,)), ((), ())), preferred_element_type=F32)
        parts = []
        for u in range(reps):
            su = jnp.zeros((tq, LANES), F32)
            for h in range(IDX_HEADS):
                su = su + wb_ref[h] * jnp.maximum(dots[h * tq:(h + 1) * tq, u * LANES:(u + 1) * LANES], 0.0)
            kpos = lane + (off + u * LANES)
            parts.append(_sortable(jnp.where(kpos <= t_b, su * IDX_SCALE, -jnp.inf)))
        keys_ref[:, pl.ds(off, cw)] = _lane_tile(parts[0], 1) if reps == 1 else jnp.concatenate(parts, axis=1)
        return 0

    lax.fori_loop(0, nch, score_body, 0)

    tau, need = _select_threshold(keys_ref, nch, cw, tq, k_top)
    tau_b = jnp.broadcast_to(tau, (tq, LANES))
    need_b = jnp.broadcast_to(need, (tq, LANES))
    pos_b = _lane_tile(t_b - lane, reps) - jnp.concatenate(
        [jnp.full((tq, LANES), u * LANES, jnp.int32) for u in range(reps)], axis=1)

    q = q_ref[0] * (HEAD_DIM ** -0.5 * LOG2_E)
    qs = []
    for n in range(KV_HEADS):
        qs.append(jnp.concatenate([q[:, (2 * n) * HEAD_DIM:(2 * n + 1) * HEAD_DIM],
                                   q[:, (2 * n + 1) * HEAD_DIM:(2 * n + 2) * HEAD_DIM]], axis=0).astype(BF16))
    m_ref[...] = jnp.full(m_ref.shape, NEG_BIG, F32)
    acc_ref[...] = jnp.zeros(acc_ref.shape, F32)

    def attn_body(c, seen):
        off = pl.multiple_of(c * cw, cw)
        bias, seen = _selection_bias(keys_ref[:, pl.ds(off, cw)], tau_b, need_b, seen, tri_ref[...])
        bias = jnp.where(off <= pos_b, bias, NEG_BIG)
        bias2 = jnp.concatenate([bias, bias], axis=0)
        ss = []
        for n in range(KV_HEADS):
            kn = k_ref[0, pl.ds(off, cw), n * HEAD_DIM:(n + 1) * HEAD_DIM]
            ss.append(lax.dot_general(qs[n], kn, (((1,), (1,)), ((), ())), preferred_element_type=F32) + bias2)
        ps, alphas = [], []
        for n in range(KV_HEADS):
            m_prev = m_ref[n]
            m_new = jnp.maximum(m_prev, jnp.max(ss[n], axis=1, keepdims=True))
            ps.append(jnp.exp2(ss[n] - _lane_tile(m_new, reps)).astype(BF16))
            alphas.append(jnp.exp2(m_prev - m_new))
            m_ref[n] = m_new
        for n in range(KV_HEADS):
            vn = v_ref[0, pl.ds(off, cw), n * LANES:(n + 1) * LANES]
            acc_ref[n] = alphas[n] * acc_ref[n] + jnp.dot(ps[n], vn, preferred_element_type=F32)
        return seen

    lax.fori_loop(0, nch, attn_body, jnp.zeros((tq, LANES), F32))

    outs = []
    for n in range(KV_HEADS):
        acc = acc_ref[n]
        o = acc[:, :HEAD_DIM] / acc[:, HEAD_DIM:]
        outs += [o[:tq], o[tq:]]
    o_ref[0] = jnp.concatenate(outs, axis=1)


def dsa_prompt_attn(q, qi, kiwi, k_bf, v_aug, ki_bf, *, cw=512):
    B, S, _ = q.shape
    cw = min(cw, S)
    assert S % cw == 0 and cw % LANES == 0 and S % Q_BLOCK == 0
    k_top = min(TOPK_MAX, S // 4)
    blk = lambda w: pl.BlockSpec((1, Q_BLOCK, w), lambda b, j: (b, j, 0))
    res = lambda w: pl.BlockSpec((1, S, w), lambda b, j: (b, 0, 0), pipeline_mode=pl.Buffered(1))
    return pl.pallas_call(
        functools.partial(_dsa_prompt_kernel, k_top=k_top, cw=cw),
        grid=(B, S // Q_BLOCK),
        in_specs=[blk(ATT_W), blk(IDX_HEADS * IDX_DIM), blk(LANES), res(KV_W), res(2 * KV_W), res(IDX_DIM),
                  pl.BlockSpec((cw, cw), lambda b, j: (0, 0), pipeline_mode=pl.Buffered(1))],
        out_specs=blk(ATT_W),
        out_shape=jax.ShapeDtypeStruct((B, S, ATT_W), F32),
        scratch_shapes=[pltpu.VMEM((Q_BLOCK, S), jnp.int32),
                        pltpu.VMEM((IDX_HEADS, Q_BLOCK, LANES), F32),
                        pltpu.VMEM((KV_HEADS, 2 * Q_BLOCK, LANES), F32),
                        pltpu.VMEM((KV_HEADS, 2 * Q_BLOCK, LANES), F32)],
        compiler_params=_cparams("parallel", "arbitrary"),
        name="dsa_prompt",
    )(q, qi, kiwi, k_bf, v_aug, ki_bf, _prefix_ones(cw))


def _dsa_sample_kernel(pt_ref, q_ref, qi_ref, kiwi_ref, kn_ref, vn_ref, tri_ref, ck_hbm, cv_hbm, cik_hbm, o_ref,
                       kbuf2, vbuf2, ibuf2, sems, keys_ref, s_ref, *, n_pages, k_top, cw):
    b = pl.program_id(0)
    n_b = pl.num_programs(0)
    t_new = q_ref.shape[1]
    past = n_pages * PAGE_SIZE
    ppc = cw // PAGE_SIZE
    nch = n_pages // ppc
    reps = cw // LANES
    slot = b % 2

    streams = ((cik_hbm, ibuf2), (ck_hbm, kbuf2), (cv_hbm, vbuf2))

    def page_copy(which, seq, sl, p):
        src, dst = streams[which]
        return pltpu.make_async_copy(src.at[pt_ref[seq, p]], dst.at[sl, p], sems.at[sl, which])

    def start_pages(seq, sl):
        def body(p, _):
            for which in range(len(streams)):
                page_copy(which, seq, sl, p).start()
            return 0
        lax.fori_loop(0, n_pages, body, 0)

    def wait_pages(which):
        def body(p, _):
            page_copy(which, b, slot, p).wait()
            return 0
        lax.fori_loop(0, n_pages, body, 0)

    @pl.when(b == 0)
    def _():
        start_pages(b, slot)

    @pl.when(b + 1 < n_b)
    def _():
        start_pages(b + 1, 1 - slot)

    ibuf = ibuf2.at[slot]
    kbuf = kbuf2.at[slot]
    vbuf = vbuf2.at[slot]

    lane = lax.broadcasted_iota(jnp.int32, (t_new, LANES), 1)
    t_b = lax.broadcasted_iota(jnp.int32, (t_new, LANES), 0)
    qi = qi_ref[0]
    qi4 = jnp.concatenate([qi[:, h * IDX_DIM:(h + 1) * IDX_DIM] for h in range(IDX_HEADS)], axis=0).astype(BF16)
    kiwi = kiwi_ref[0]
    wi = kiwi[:, IDX_DIM:IDX_DIM + IDX_HEADS]
    wb = [jnp.broadcast_to(wi[:, h:h + 1], (t_new, LANES)) for h in range(IDX_HEADS)]

    def combine(dots_u):
        su = jnp.zeros((t_new, LANES), F32)
        for h in range(IDX_HEADS):
            su = su + wb[h] * jnp.maximum(dots_u[h * t_new:(h + 1) * t_new], 0.0)
        return su * IDX_SCALE

    wait_pages(0)

    def score_body(c, _):
        kic = ibuf[pl.ds(c * ppc, ppc)].reshape(cw, IDX_DIM).astype(BF16)
        dots = lax.dot_general(qi4, kic, _NT, preferred_element_type=F32)
        parts = [_sortable(combine(dots[:, u * LANES:(u + 1) * LANES])) for u in range(reps)]
        keys_ref[:, pl.ds(pl.multiple_of(c * cw, cw), cw)] = jnp.concatenate(parts, axis=1)
        return 0

    lax.fori_loop(0, nch, score_body, 0)
    ki_new = jnp.concatenate([kiwi[:, :IDX_DIM], jnp.zeros((LANES - t_new, IDX_DIM), F32)], axis=0).astype(BF16)
    dots_new = lax.dot_general(qi4, ki_new, _NT, preferred_element_type=F32)
    s_new = jnp.where(lane <= t_b, combine(dots_new), -jnp.inf)
    neg = jnp.full((t_new, LANES), -jnp.inf, F32)
    keys_ref[:, pl.ds(past, cw)] = _sortable(jnp.concatenate([s_new] + [neg] * (reps - 1), axis=1))

    tau, need = _select_threshold(keys_ref, nch + 1, cw, t_new, k_top)
    tau_b = jnp.broadcast_to(tau, (t_new, LANES))
    need_b = jnp.broadcast_to(need, (t_new, LANES))

    q = q_ref[0] * (HEAD_DIM ** -0.5 * LOG2_E)
    qs = []
    for n in range(KV_HEADS):
        qs.append(jnp.concatenate([q[:, (2 * n) * HEAD_DIM:(2 * n + 1) * HEAD_DIM],
                                   q[:, (2 * n + 1) * HEAD_DIM:(2 * n + 2) * HEAD_DIM]], axis=0).astype(BF16))
    rows2 = 2 * t_new

    def logits(kc_bf, bias, off, mx):
        width = bias.shape[1]
        bias2 = jnp.concatenate([bias, bias], axis=0)
        new_mx = []
        for n in range(KV_HEADS):
            s = lax.dot_general(qs[n], kc_bf[:, n * HEAD_DIM:(n + 1) * HEAD_DIM], _NT, preferred_element_type=F32) + bias2
            s_ref[n, :, pl.ds(off, width)] = s
            m = mx[n]
            for u in range(width // LANES):
                m = jnp.maximum(m, s[:, u * LANES:(u + 1) * LANES])
            new_mx.append(m)
        return tuple(new_mx)

    wait_pages(1)

    def logit_body(c, carry):
        mx, seen = carry
        off = pl.multiple_of(c * cw, cw)
        kc = kbuf[pl.ds(c * ppc, ppc)].reshape(cw, KV_W).astype(BF16)
        bias, seen = _selection_bias(keys_ref[:, pl.ds(off, cw)], tau_b, need_b, seen, tri_ref[...])
        return logits(kc, bias, off, mx), seen

    mx, seen = lax.fori_loop(0, nch, logit_body,
                             (tuple(jnp.full((rows2, LANES), NEG_BIG, F32) for _ in range(KV_HEADS)),
                              jnp.zeros((t_new, LANES), F32)))
    pad_rows = jnp.zeros((LANES - t_new, KV_W), F32)
    kn_pad = jnp.concatenate([kn_ref[0], pad_rows], axis=0).astype(BF16)
    vn_pad = jnp.concatenate([vn_ref[0], pad_rows], axis=0).astype(BF16)
    bias_new, _ = _selection_bias(keys_ref[:, pl.ds(past, LANES)], tau_b, need_b, seen, tri_ref[:LANES, :LANES])
    mx = logits(kn_pad, jnp.where(lane <= t_b, bias_new, NEG_BIG), past, mx)
    m_row = [jnp.broadcast_to(jnp.max(m, axis=1, keepdims=True), (rows2, LANES)) for m in mx]

    def weighted(vc_bf, off, width, carry):
        accs, sums = carry
        new_accs, new_sums = [], []
        for n in range(KV_HEADS):
            p = jnp.exp2(s_ref[n, :, pl.ds(off, width)] - _lane_tile(m_row[n], width // LANES))
            l = sums[n]
            for u in range(width // LANES):
                l = l + p[:, u * LANES:(u + 1) * LANES]
            new_sums.append(l)
            new_accs.append(accs[n] + jnp.dot(p.astype(BF16), vc_bf[:, n * HEAD_DIM:(n + 1) * HEAD_DIM],
                                              preferred_element_type=F32))
        return tuple(new_accs), tuple(new_sums)

    wait_pages(2)

    def pv_body(c, carry):
        vc = vbuf[pl.ds(c * ppc, ppc)].reshape(cw, KV_W).astype(BF16)
        return weighted(vc, pl.multiple_of(c * cw, cw), cw, carry)

    zero = lambda w: tuple(jnp.zeros((rows2, w), F32) for _ in range(KV_HEADS))
    carry = lax.fori_loop(0, nch, pv_body, (zero(HEAD_DIM), zero(LANES)))
    accs, sums = weighted(vn_pad, past, LANES, carry)

    outs = []
    for n in range(KV_HEADS):
        o = accs[n] / jnp.sum(sums[n], axis=1, keepdims=True)
        outs += [o[:t_new], o[t_new:]]
    o_ref[0] = jnp.concatenate(outs, axis=1)


def dsa_sample_attn(q, qi, kiwi, k_new, v_new, cache_k, cache_v, cache_idx_k, page_table, *, cw=512):
    Bd, T, _ = q.shape
    n_pages = page_table.shape[1]
    past = n_pages * PAGE_SIZE
    cw = min(cw, past)
    assert cw % PAGE_SIZE == 0 and past % cw == 0 and T <= LANES
    k_top = min(TOPK_MAX, (past + T) // 4)
    n_phys = cache_k.shape[0]
    blk = lambda w: pl.BlockSpec((1, T, w), lambda b, pt: (b, 0, 0))
    hbm = pl.BlockSpec(memory_space=pl.ANY)
    gs = pltpu.PrefetchScalarGridSpec(
        num_scalar_prefetch=1,
        grid=(Bd,),
        in_specs=[blk(ATT_W), blk(IDX_HEADS * IDX_DIM), blk(LANES), blk(KV_W), blk(KV_W),
                  pl.BlockSpec((cw, cw), lambda b, pt: (0, 0)), hbm, hbm, hbm],
        out_specs=blk(ATT_W),
        scratch_shapes=[pltpu.VMEM((2, n_pages, PAGE_SIZE, KV_W), F32),
                        pltpu.VMEM((2, n_pages, PAGE_SIZE, KV_W), F32),
                        pltpu.VMEM((2, n_pages, PAGE_SIZE, IDX_DIM), F32),
                        pltpu.SemaphoreType.DMA((2, 3)),
                        pltpu.VMEM((T, past + cw), jnp.int32),
                        pltpu.VMEM((KV_HEADS, 2 * T, past + LANES), F32)])
    return pl.pallas_call(
        functools.partial(_dsa_sample_kernel, n_pages=n_pages, k_top=k_top, cw=cw),
        grid_spec=gs,
        out_shape=jax.ShapeDtypeStruct((Bd, T, ATT_W), F32),
        compiler_params=_cparams("arbitrary"),
        name="dsa_sample",
    )(page_table, q, qi, kiwi, k_new, v_new, _prefix_ones(cw),
      cache_k.reshape(n_phys, PAGE_SIZE, KV_W), cache_v.reshape(n_phys, PAGE_SIZE, KV_W), cache_idx_k)


def _cross_attn_kernel(x_ref, g_ref, wq_ref, mk_ref, mv_ref, wo_ref, o_ref):
    x = x_ref[0]
    q = jnp.dot(_rms(x, g_ref[...]).astype(BF16), wq_ref[...], preferred_element_type=F32) * (MEM_HD ** -0.5)
    q = q.astype(BF16)
    mk = mk_ref[0].astype(BF16)
    mv = mv_ref[0].astype(BF16)
    outs = []
    for h in range(MEM_HEADS):
        sl = slice(h * MEM_HD, (h + 1) * MEM_HD)
        s = lax.dot_general(q[:, sl], mk[:, sl], _NT, preferred_element_type=F32)
        p = jnp.exp(s - jnp.max(s, axis=1, keepdims=True))
        p = p / jnp.sum(p, axis=1, keepdims=True)
        outs.append(jnp.dot(p.astype(BF16), mv[:, sl], preferred_element_type=F32))
    o = jnp.concatenate(outs, axis=1).astype(BF16)
    o_ref[0] = x + jnp.dot(o, wo_ref[...], preferred_element_type=F32)


def cross_attn_mem(x, norm_g, wq_bf, mk, mv, wo_bf, *, tm=512):
    B, T, _ = x.shape
    tm = min(tm, T)
    assert T % tm == 0
    M = mk.shape[1]
    full = lambda a: pl.BlockSpec(a.shape, lambda b, i: (0,) * a.ndim)
    mem = pl.BlockSpec((1, M, MEM_W), lambda b, i: (b, 0, 0))
    return pl.pallas_call(
        _cross_attn_kernel,
        grid=(B, T // tm),
        in_specs=[pl.BlockSpec((1, tm, D_MODEL), lambda b, i: (b, i, 0)), full(norm_g), full(wq_bf), mem, mem, full(wo_bf)],
        out_specs=pl.BlockSpec((1, tm, D_MODEL), lambda b, i: (b, i, 0)),
        out_shape=jax.ShapeDtypeStruct(x.shape, F32),
        compiler_params=_cparams("parallel", "arbitrary"),
        name="cross_attn",
    )(x, norm_g, wq_bf, mk, mv, wo_bf)


def _norm_proj_kernel(x_ref, g_ref, w_ref, o_ref):
    o_ref[...] = jnp.dot(_rms(x_ref[...], g_ref[...]).astype(BF16), w_ref[...], preferred_element_type=F32)


def norm_proj(x2d, norm_g, w_bf, *, tm=256):
    n = x2d.shape[0]
    tm = min(tm, n)
    assert n % tm == 0
    return pl.pallas_call(
        _norm_proj_kernel,
        grid=(n // tm,),
        in_specs=[pl.BlockSpec((tm, x2d.shape[1]), lambda i: (i, 0)), pl.BlockSpec(norm_g.shape, lambda i: (0, 0)),
                  pl.BlockSpec(w_bf.shape, lambda i: (0, 0))],
        out_specs=pl.BlockSpec((tm, w_bf.shape[1]), lambda i: (i, 0)),
        out_shape=jax.ShapeDtypeStruct((n, w_bf.shape[1]), F32),
        compiler_params=_cparams("parallel"),
        name="norm_proj",
    )(x2d, norm_g, w_bf)


PEER_HALF = PEER_DK // 2
PEER_SUB = 256
PEER_TE = 2 * PEER_SUB
PEER_CAND_CAP = tuple(PEER_TOPK // (a + 1) for a in range(PEER_TOPK))
PEER_CAND_ROWS = -(-sum(PEER_CAND_CAP) // 8) * 8


def _extract_top(work_ref, n_keys, tn, on_pick):
    sub = lax.broadcasted_iota(jnp.int32, (n_keys, tn), 0).astype(F32)
    for a in range(PEER_TOPK):
        w = work_ref[...]
        m = jnp.max(w, axis=0, keepdims=True)
        idx = jnp.min(jnp.where(w == m, sub, float(n_keys)), axis=0, keepdims=True)
        hit = sub == idx
        work_ref[...] = jnp.where(hit, -jnp.inf, w)
        on_pick(a, m, hit)


def _peer_kernel(x_ref, g_ref, gf_ref, wq_ref, k1_ref, k2_ref, u_ref, vt_ref, o_ref,
                 xn_ref, s_ref, rank_ref, vals_ref, work_ref, cand_ref, c_ref, l_ref, e2_ref, acc_ref, ht_ref, w_ref,
                 *, tn, final_norm):
    e = pl.program_id(1)
    n_e = pl.num_programs(1)

    @pl.when(e == 0)
    def _route():
        xn = _rms(x_ref[...], g_ref[...]).astype(BF16)
        xn_ref[...] = xn
        q = jnp.dot(xn, wq_ref[...], preferred_element_type=F32).astype(BF16)
        for h in range(PEER_HEADS):
            q1 = q[:, h * PEER_DK:h * PEER_DK + PEER_HALF]
            q2 = q[:, h * PEER_DK + PEER_HALF:(h + 1) * PEER_DK]
            nt = (((1,), (1,)), ((), ()))
            s_ref[h] = lax.dot_general(k1_ref[h], q1, nt, preferred_element_type=F32)
            s_ref[PEER_HEADS + h] = lax.dot_general(k2_ref[h], q2, nt, preferred_element_type=F32)

        def top_keys(hh, _):
            work_ref[...] = s_ref[hh]
            rank_ref[hh] = jnp.full((N_KEYS, tn), float(PEER_TOPK), F32)

            def pick(a, m, hit):
                vals_ref[hh, a:a + 1, :] = m
                rank_ref[hh] = jnp.where(hit, float(a), rank_ref[hh])
            _extract_top(work_ref, N_KEYS, tn, pick)
            return 0

        lax.fori_loop(0, 2 * PEER_HEADS, top_keys, 0)

        def gates(h, _):
            v1 = vals_ref[h]
            v2 = vals_ref[PEER_HEADS + h]
            rows = [v1[a:a + 1, :] + v2[0:cap, :] for a, cap in enumerate(PEER_CAND_CAP)]
            rows.append(jnp.full((PEER_CAND_ROWS - sum(PEER_CAND_CAP), tn), -jnp.inf, F32))
            cand_ref[...] = jnp.concatenate(rows, axis=0)
            _extract_top(cand_ref, PEER_CAND_ROWS, tn, lambda a, m, hit: None)
            picked = jnp.where(cand_ref[...] == -jnp.inf, 1.0, 0.0)
            e1 = jnp.exp(v1 - v1[0:1, :])
            e2 = jnp.exp(v2 - v2[0:1, :])
            z = jnp.zeros((1, tn), F32)
            r1 = rank_ref[h]
            lfull = jnp.zeros((N_KEYS, tn), F32)
            off = 0
            for a, cap in enumerate(PEER_CAND_CAP):
                sel = picked[off:off + cap, :]
                off += cap
                z = z + e1[a:a + 1, :] * jnp.sum(sel * e2[0:cap, :], axis=0, keepdims=True)
                lfull = lfull + jnp.where(r1 == float(a), jnp.sum(sel, axis=0, keepdims=True), 0.0)
            l_ref[h] = lfull
            c_ref[h] = jnp.where(r1 < float(PEER_TOPK), jnp.exp(s_ref[h] - v1[0:1, :]) / z, 0.0)
            e2_ref[h] = jnp.exp(s_ref[PEER_HEADS + h] - v2[0:1, :])
            return 0

        lax.fori_loop(0, PEER_HEADS, gates, 0)
        acc_ref[...] = jnp.zeros(acc_ref.shape, F32)

        ht_ref[...] = jnp.zeros(ht_ref.shape, F32)
        w_ref[...] = jnp.zeros(w_ref.shape, BF16)

    n_sub = 2 * (n_e - 1)
    keys_per_sub = PEER_SUB // N_KEYS

    def gated(ht, sub):
        live = jnp.where(jnp.logical_and(sub >= 0, sub < n_sub), 1.0, 0.0)
        base = jnp.clip(sub, 0, n_sub - 1) * keys_per_sub
        parts = []
        for ii in range(keys_per_sub):
            g = jnp.zeros((N_KEYS, tn), F32)
            for h in range(PEER_HEADS):
                cb = c_ref[h, pl.ds(base + ii, 1), :] * live
                lb = l_ref[h, pl.ds(base + ii, 1), :]
                g = g + cb * jnp.where(rank_ref[PEER_HEADS + h] < lb, e2_ref[h], 0.0)
            hi = ht[ii * N_KEYS:(ii + 1) * N_KEYS]
            act = 0.5 * hi * (1.0 + lax.erf(hi * (2.0 ** -0.5)))
            parts.append((g * act).astype(BF16))
        return parts[0] if len(parts) == 1 else jnp.concatenate(parts, axis=0)

    nt = (((1,), (1,)), ((), ()))
    acc_ref[...] += jnp.dot(vt_ref[:, :PEER_SUB], w_ref[...], preferred_element_type=F32)
    ht_new = lax.dot_general(u_ref[:PEER_SUB, :], xn_ref[...], nt, preferred_element_type=F32)
    w_mid = gated(ht_ref[...], 2 * e - 1)
    ht_ref[...] = lax.dot_general(u_ref[PEER_SUB:, :], xn_ref[...], nt, preferred_element_type=F32)
    w_ref[...] = gated(ht_new, 2 * e)
    acc_ref[...] += jnp.dot(vt_ref[:, PEER_SUB:], w_mid, preferred_element_type=F32)

    @pl.when(e == n_e - 1)
    def _finish():
        y = x_ref[...] + acc_ref[...].T
        if final_norm:
            y = _rms(y, gf_ref[...])
        o_ref[...] = y


def peer_ffn(x2d, norm_g, final_g, wq_bf, k1_bf, k2_bf, u_bf, vt_bf, *, final_norm, tn=512):
    n = x2d.shape[0]
    tn = min(tn, n)
    assert n % tn == 0 and tn % LANES == 0
    n_blk = u_bf.shape[0] // PEER_TE
    full = lambda a: pl.BlockSpec(a.shape, lambda t, e: (0,) * a.ndim)
    return pl.pallas_call(
        functools.partial(_peer_kernel, tn=tn, final_norm=final_norm),
        grid=(n // tn, n_blk + 1),
        in_specs=[pl.BlockSpec((tn, D_MODEL), lambda t, e: (t, 0)), full(norm_g), full(final_g), full(wq_bf),
                  full(k1_bf), full(k2_bf),
                  pl.BlockSpec((PEER_TE, D_MODEL), lambda t, e: (jnp.minimum(e, n_blk - 1), 0)),
                  pl.BlockSpec((D_MODEL, PEER_TE), lambda t, e: (0, jnp.maximum(e - 1, 0)))],
        out_specs=pl.BlockSpec((tn, D_MODEL), lambda t, e: (t, 0)),
        out_shape=jax.ShapeDtypeStruct((n, D_MODEL), F32),
        scratch_shapes=[pltpu.VMEM((tn, D_MODEL), BF16),
                        pltpu.VMEM((2 * PEER_HEADS, N_KEYS, tn), F32),
                        pltpu.VMEM((2 * PEER_HEADS, N_KEYS, tn), F32),
                        pltpu.VMEM((2 * PEER_HEADS, PEER_TOPK, tn), F32),
                        pltpu.VMEM((N_KEYS, tn), F32),
                        pltpu.VMEM((PEER_CAND_ROWS, tn), F32),
                        pltpu.VMEM((PEER_HEADS, N_KEYS, tn), F32),
                        pltpu.VMEM((PEER_HEADS, N_KEYS, tn), F32),
                        pltpu.VMEM((PEER_HEADS, N_KEYS, tn), F32),
                        pltpu.VMEM((D_MODEL, tn), F32),
                        pltpu.VMEM((PEER_SUB, tn), F32),
                        pltpu.VMEM((PEER_SUB, tn), BF16)],
        compiler_params=_cparams("parallel", "arbitrary"),
        name="peer_ffn",
    )(x2d, norm_g, final_g, wq_bf, k1_bf, k2_bf, u_bf, vt_bf)


RWKV_GROUP = 4
RWKV_GW = RWKV_GROUP * RWKV_N
RWKV_CHUNK = 64


def _split_bf16(x, terms):
    parts = []
    for _ in range(terms):
        p = x.astype(BF16)
        parts.append(p)
        x = x - p.astype(F32)
    return parts


def _dot3(a, b, dims=(((1,), (0,)), ((), ()))):
    ah, al = _split_bf16(a, 2)
    bh, bl = _split_bf16(b, 2)
    d = lambda x, y: lax.dot_general(x, y, dims, preferred_element_type=F32)
    return d(ah, bh) + d(al, bh) + d(ah, bl)


def _dot1(a, b, dims=(((1,), (0,)), ((), ()))):
    return lax.dot_general(a.astype(BF16), b.astype(BF16), dims, preferred_element_type=F32)


def _dot_exact_rhs(a, b_exact_bf16):
    return sum(jnp.dot(p, b_exact_bf16, preferred_element_type=F32) for p in _split_bf16(a, 3))


def _dot_exact_lhs(a_exact_bf16, b):
    return sum(jnp.dot(a_exact_bf16, p, preferred_element_type=F32) for p in _split_bf16(b, 3))


_NT = (((1,), (1,)), ((), ()))
_TN = (((0,), (0,)), ((), ()))


def _rwkv_kernel(feat_ref, prev_ref, s0t_ref, mu_ref, w0_ref, w2_ref, a0_ref, a2_ref, g2_ref, kk_ref, ka_ref, rk_ref,
                 lng_ref, lnb_ref, hsum_ref, tri_ref, o_ref, sfin_ref, state_ref, carry_ref, *, c_len, n_double):
    c = pl.program_id(1)
    n_c = pl.num_programs(1)
    n4 = RWKV_GROUP * c_len

    @pl.when(c == 0)
    def _init():
        carry_ref[0:1, :] = prev_ref[0]
        state_ref[...] = jnp.zeros(state_ref.shape, F32)
        for h in range(RWKV_HEADS):
            g, hh = divmod(h, RWKV_GROUP)
            state_ref[g, hh * RWKV_N:(hh + 1) * RWKV_N, hh * RWKV_N:(hh + 1) * RWKV_N] = s0t_ref[0, h]

    feat = feat_ref[0]
    row = lax.broadcasted_iota(jnp.int32, feat.shape, 0)
    shifted = jnp.where(row == 0, jnp.broadcast_to(carry_ref[0:1, :], feat.shape), pltpu.roll(feat, 1, 0))
    carry_ref[0:1, :] = feat[c_len - 1:c_len, :]
    mixed = feat + (shifted - feat) * mu_ref[...]
    r = mixed[:, 0:RWKV_W]
    k = mixed[:, RWKV_W:2 * RWKV_W]
    v = mixed[:, 2 * RWKV_W:3 * RWKV_W]
    o1 = 3 * RWKV_W
    wd = mixed[:, o1:o1 + LORA_W]
    ad = mixed[:, o1 + LORA_W:o1 + LORA_W + LORA_A]
    gd = mixed[:, o1 + LORA_W + LORA_A:]

    zw = -(w0_ref[...] + _dot3(jnp.tanh(wd), w2_ref[...]))
    log_w = -(jnp.maximum(zw, 0.0) + jnp.log1p(jnp.exp(-jnp.abs(zw)))) - 0.5
    dlog = -jnp.exp(log_w)
    a = jax.nn.sigmoid(a0_ref[...] + _dot3(ad, a2_ref[...]))
    gate = _dot3(jax.nn.sigmoid(gd), g2_ref[...])
    hsum = hsum_ref[...]
    kk = k * kk_ref[...]
    kk = kk / jnp.maximum(jnp.sqrt(_dot_exact_rhs(kk * kk, hsum)), 1e-12)
    k_eff = k * (1.0 + (a - 1.0) * ka_ref[...])

    cum = _dot_exact_lhs(tri_ref[...], dlog)
    g_in = jnp.exp(cum)
    g_inv = jnp.exp(-cum)
    kk_t = kk * jnp.exp(cum - dlog)
    b_h = kk * a * g_inv
    k_h = k_eff * g_inv
    r_t = r * g_in
    g_last = g_in[c_len - 1:c_len, :]

    rr = lax.broadcasted_iota(jnp.int32, (n4, n4), 0)
    cc = lax.broadcasted_iota(jnp.int32, (n4, n4), 1)
    eye = jnp.where(rr == cc, 1.0, 0.0)
    lane_g = lax.broadcasted_iota(jnp.int32, (c_len, RWKV_GW), 1) // RWKV_N
    rr_g = lax.broadcasted_iota(jnp.int32, (RWKV_GW, RWKV_GW), 0)
    cc_g = lax.broadcasted_iota(jnp.int32, (RWKV_GW, RWKV_GW), 1)

    groups = range(RWKV_HEADS // RWKV_GROUP)
    each = lambda f: [f(g) for g in groups]

    def stack(x, g):
        xg = x[:, g * RWKV_GW:(g + 1) * RWKV_GW]
        return jnp.concatenate([jnp.where(lane_g == h, xg, 0.0) for h in range(RWKV_GROUP)], axis=0)

    kks, rs, bs = each(lambda g: stack(kk_t, g)), each(lambda g: stack(r_t, g)), each(lambda g: stack(b_h, g))
    ks, vs = each(lambda g: stack(k_h, g)), each(lambda g: stack(v, g))
    l_b = each(lambda g: jnp.where(rr > cc, _dot3(kks[g], bs[g], _NT), 0.0))
    l_k = each(lambda g: jnp.where(rr > cc, _dot3(kks[g], ks[g], _NT), 0.0))
    m_b = each(lambda g: jnp.where(rr >= cc, _dot1(rs[g], bs[g], _NT), 0.0))
    m_k = each(lambda g: jnp.where(rr >= cc, _dot1(rs[g], ks[g], _NT), 0.0))
    npow = each(lambda g: -l_b[g])
    t_inv = each(lambda g: eye + npow[g])
    for _ in range(n_double):
        npow = each(lambda g: _dot3(npow[g], npow[g]))
        t_inv = each(lambda g: t_inv[g] + _dot3(t_inv[g], npow[g]))
    w1 = each(lambda g: -_dot3(t_inv[g], kks[g]))
    lkv = each(lambda g: _dot3(l_k[g], vs[g]))
    u1 = each(lambda g: -_dot3(t_inv[g], lkv[g]))
    gl = each(lambda g: g_last[:, g * RWKV_GW:(g + 1) * RWKV_GW])
    bsg = each(lambda g: bs[g] * gl[g])
    ksg = each(lambda g: ks[g] * gl[g])
    g_mat = each(lambda g: jnp.where(rr_g == cc_g, jnp.broadcast_to(gl[g], (RWKV_GW, RWKV_GW)), 0.0)
                 + _dot3(bsg[g], w1[g], _TN))
    h_mat = each(lambda g: _dot3(bsg[g], u1[g], _TN) + _dot3(ksg[g], vs[g], _TN))
    r_y = each(lambda g: rs[g] + _dot1(m_b[g], w1[g]))
    y_0 = each(lambda g: _dot1(m_b[g], u1[g]) + _dot1(m_k[g], vs[g]))
    s_t = each(lambda g: state_ref[g])
    y_st = each(lambda g: _dot1(r_y[g], s_t[g]) + y_0[g])
    s_new = each(lambda g: _dot3(g_mat[g], s_t[g]) + h_mat[g])
    for g in groups:
        state_ref[g] = s_new[g]
    ys = each(lambda g: sum(y_st[g][h * c_len:(h + 1) * c_len] for h in range(RWKV_GROUP)))
    y = jnp.concatenate(ys, axis=1)

    inv_n = 1.0 / RWKV_N
    mean = _dot_exact_rhs(y, hsum) * inv_n
    yc = y - mean
    var = _dot_exact_rhs(yc * yc, hsum) * inv_n
    yn = yc * lax.rsqrt(var + LN_X_EPS) * lng_ref[...] + lnb_ref[...]
    bonus = _dot_exact_rhs(r * k_eff * rk_ref[...], hsum) * v
    o_ref[0] = (yn + bonus) * gate

    @pl.when(c == n_c - 1)
    def _fin():
        for h in range(RWKV_HEADS):
            g, hh = divmod(h, RWKV_GROUP)
            sfin_ref[0, h] = state_ref[g, hh * RWKV_N:(hh + 1) * RWKV_N, hh * RWKV_N:(hh + 1) * RWKV_N]


def rwkv_mix(feat, prev, s0, lw, *, c_len):
    B, T, _ = feat.shape
    assert T % c_len == 0 and c_len % 8 == 0
    n_double = max(0, (c_len - 1).bit_length() - 1)
    hsum = (jnp.arange(RWKV_W)[:, None] // RWKV_N == jnp.arange(RWKV_W)[None, :] // RWKV_N).astype(BF16)
    tri = (jnp.arange(c_len)[:, None] >= jnp.arange(c_len)[None, :]).astype(BF16)
    row = lambda a: a.reshape(1, -1).astype(F32)
    params = [row(lw['rwkv_mu']), row(lw['rwkv_w0']), lw['rwkv_w2'], row(lw['rwkv_a0']), lw['rwkv_a2'], lw['rwkv_g2'],
              row(lw['rwkv_k_k']), row(lw['rwkv_k_a']), row(lw['rwkv_r_k']), row(lw['rwkv_ln_g']), row(lw['rwkv_ln_b']),
              hsum, tri]
    full = lambda a: pl.BlockSpec(a.shape, lambda b, c: (0,) * a.ndim)
    st_spec = pl.BlockSpec((1, RWKV_HEADS, RWKV_N, RWKV_N), lambda b, c: (b, 0, 0, 0))
    out, s_fin_t = pl.pallas_call(
        functools.partial(_rwkv_kernel, c_len=c_len, n_double=n_double),
        grid=(B, T // c_len),
        in_specs=[pl.BlockSpec((1, c_len, SHIFT_W), lambda b, c: (b, c, 0)),
                  pl.BlockSpec((1, 1, SHIFT_W), lambda b, c: (b, 0, 0)), st_spec] + [full(p) for p in params],
        out_specs=[pl.BlockSpec((1, c_len, RWKV_W), lambda b, c: (b, c, 0)), st_spec],
        out_shape=[jax.ShapeDtypeStruct((B, T, RWKV_W), F32),
                   jax.ShapeDtypeStruct((B, RWKV_HEADS, RWKV_N, RWKV_N), F32)],
        scratch_shapes=[pltpu.VMEM((RWKV_HEADS // RWKV_GROUP, RWKV_GW, RWKV_GW), F32),
                        pltpu.VMEM((8, SHIFT_W), F32)],
        compiler_params=_cparams("parallel", "arbitrary"),
        name="rwkv_mix",
    )(feat, prev[:, None, :], jnp.swapaxes(s0, -1, -2), *params)
    return out, jnp.swapaxes(s_fin_t, -1, -2)


def _rope_inv_tiled():
    half = HEAD_DIM // 2
    inv = ROPE_THETA ** (-jnp.arange(half, dtype=jnp.float32) / half)
    return jnp.tile(inv, LANES // half)[None, :]


def trunk_layer(x, period, offset, lw, paged, shift_prev, wkv0, mk, mv, final_g, is_last):
    B, T, _ = x.shape
    n = B * T
    x2d = x.reshape(n, D_MODEL)
    proj = in_proj(x2d, lw['norm_mix'], lw['w_in_packed'], _rope_inv_tiled(), period=period, offset=offset,
                   attn_operands=paged is None)
    q, k, v, qi, kiwi, feat, ga, gb = proj[:8]
    b3 = lambda a: a.reshape(B, T, a.shape[-1])
    ki = kiwi[:, :IDX_DIM].reshape(B, T, IDX_DIM)
    feat = b3(feat)
    if paged is None:
        k_bf, ki_bf, v_aug = proj[8:]
        o_a = dsa_prompt_attn(b3(q), b3(qi), b3(kiwi), b3(k_bf), b3(v_aug), b3(ki_bf))
    else:
        o_a = dsa_sample_attn(b3(q), b3(qi), b3(kiwi), b3(k), b3(v), *paged)
    o_b, wkv_fin = rwkv_mix(feat, shift_prev, wkv0, lw, c_len=min(T, RWKV_CHUNK))
    shift_last = feat[:, -1]
    x2d = mix_out(x2d, o_a.reshape(n, ATT_W), o_b.reshape(n, RWKV_W), ga, gb, lw['proj_a'], lw['proj_b'], lw['w_out'])
    x = cross_attn_mem(x2d.reshape(B, T, D_MODEL), lw['norm_ca'], lw['w_cq'], mk, mv, lw['w_co'])
    x2d = peer_ffn(x.reshape(n, D_MODEL), lw['norm_ffn'], final_g, lw['peer_wq'], lw['peer_k1'],
                   lw['peer_k2'], lw['peer_u'], lw['peer_vt'], final_norm=is_last)
    new_k = k.reshape(B, T, KV_HEADS, HEAD_DIM)
    new_v = v.reshape(B, T, KV_HEADS, HEAD_DIM)
    return x2d.reshape(B, T, D_MODEL), (new_k, new_v, ki, shift_last, wkv_fin)


def kernel(x_prompt, x_sample, cache_k, cache_v, cache_idx_k, state_shift, state_wkv, cache_mem_k, cache_mem_v, page_table, mem_prompt, norm_mix, w_in, rwkv_mu, rwkv_w0, rwkv_w2, rwkv_a0, rwkv_a2, rwkv_g2, rwkv_k_k, rwkv_k_a, rwkv_r_k, rwkv_ln_g, rwkv_ln_b, proj_a, proj_b, w_out, norm_ca, norm_mem, w_cq, w_mk, w_mv, w_co, norm_ffn, peer_wq, peer_k1, peer_k2, peer_u, peer_v, norm_final):
    B, S, _ = x_prompt.shape
    Bd, T, _ = x_sample.shape
    depth = w_in.shape[0]
    past = page_table.shape[1] * PAGE_SIZE
    xp, xs = x_prompt, x_sample
    st_p_all, st_s_all, mem_all = [], [], []
    bf = lambda a: a.astype(BF16)
    row = lambda a: a.reshape(1, -1)
    final_g = row(norm_final)
    M = mem_prompt.shape[1]
    for l in range(depth):
        lw = {
            'norm_mix': row(norm_mix[l]), 'w_in_packed': _pack_w_in(w_in[l]), 'rwkv_mu': rwkv_mu[l], 'rwkv_w0': rwkv_w0[l],
            'rwkv_w2': rwkv_w2[l], 'rwkv_a0': rwkv_a0[l], 'rwkv_a2': rwkv_a2[l], 'rwkv_g2': rwkv_g2[l],
            'rwkv_k_k': rwkv_k_k[l], 'rwkv_k_a': rwkv_k_a[l], 'rwkv_r_k': rwkv_r_k[l], 'rwkv_ln_g': rwkv_ln_g[l],
            'rwkv_ln_b': rwkv_ln_b[l], 'proj_a': bf(proj_a[l]), 'proj_b': bf(proj_b[l]), 'w_out': bf(w_out[l]),
            'norm_ca': row(norm_ca[l]), 'w_cq': bf(w_cq[l]), 'w_co': bf(w_co[l]), 'norm_ffn': row(norm_ffn[l]),
            'peer_wq': bf(peer_wq[l]), 'peer_k1': bf(peer_k1[l]), 'peer_k2': bf(peer_k2[l]),
            'peer_u': bf(peer_u[l]), 'peer_vt': bf(peer_v[l].T),
        }
        is_last = l == depth - 1
        mem_kv = norm_proj(mem_prompt.reshape(B * M, D_MODEL), row(norm_mem[l]),
                           bf(jnp.concatenate([w_mk[l], w_mv[l]], axis=1)))
        mk_p = mem_kv[:, :MEM_W].reshape(B, M, MEM_W)
        mv_p = mem_kv[:, MEM_W:].reshape(B, M, MEM_W)
        xp, st_p = trunk_layer(xp, S, 0, lw, None,
                               jnp.zeros((B, SHIFT_W), F32), jnp.zeros((B, RWKV_HEADS, RWKV_N, RWKV_N), F32),
                               mk_p, mv_p, final_g, is_last)
        paged = (cache_k[l], cache_v[l], cache_idx_k[l], page_table)
        xs, st_s = trunk_layer(xs, T, past, lw, paged, state_shift[l], state_wkv[l],
                               cache_mem_k[l].reshape(Bd, M, MEM_W), cache_mem_v[l].reshape(Bd, M, MEM_W), final_g, is_last)
        st_p_all.append(st_p)
        st_s_all.append(st_s)
        mem_all.append((mk_p.reshape(B, M, MEM_HEADS, MEM_HD), mv_p.reshape(B, M, MEM_HEADS, MEM_HD)))
    y_prompt, y_sample = xp, xs
    stack = lambda lst, i: jnp.stack([s[i] for s in lst])
    return (y_prompt, y_sample,
            stack(st_p_all, 0), stack(st_p_all, 1), stack(st_p_all, 2), stack(st_p_all, 3), stack(st_p_all, 4),
            stack(mem_all, 0), stack(mem_all, 1),
            stack(st_s_all, 0), stack(st_s_all, 1), stack(st_s_all, 2), stack(st_s_all, 3), stack(st_s_all, 4))
```

```python
import functools
import math

import jax
import jax.numpy as jnp
import numpy as np
from jax import lax
from jax.experimental import pallas as pl
from jax.experimental.pallas import tpu as pltpu

D_MODEL = 1024
PAGE_SIZE = 128
ATT_HEADS = 8
KV_HEADS = 4
HEAD_DIM = 64
ATT_W = ATT_HEADS * HEAD_DIM
KV_W = KV_HEADS * HEAD_DIM
IDX_HEADS = 4
IDX_DIM = 64
TOPK_MAX = 256
Q_BLOCK = 128
ROPE_THETA = 10000.0
RWKV_HEADS = 8
RWKV_N = 64
RWKV_W = RWKV_HEADS * RWKV_N
LORA_W = 64
LORA_A = 64
LORA_G = 128
LN_X_EPS = 64e-5
SHIFT_W = 3 * RWKV_W + LORA_W + LORA_A + LORA_G
MEM_HEADS = 4
MEM_HD = 128
MEM_W = MEM_HEADS * MEM_HD
PEER_HEADS = 8
PEER_DK = 128
N_KEYS = 128
PEER_TOPK = 16
PEER_BLOCK = 128
NORM_EPS = 1e-6
SPLITS = (ATT_W, KV_W, KV_W, IDX_HEADS * IDX_DIM, IDX_DIM, IDX_HEADS, SHIFT_W, 2 * D_MODEL)

LANES = 128
VMEM_LIMIT = 56 * 1024 * 1024

F32 = jnp.float32
BF16 = jnp.bfloat16


def _cparams(*sem):
    return pltpu.CompilerParams(dimension_semantics=sem, vmem_limit_bytes=VMEM_LIMIT)


def _rms(x, g):
    return x * lax.rsqrt(jnp.mean(x * x, axis=-1, keepdims=True) + NORM_EPS) * g


_SEG_Q = (0, 512)
_SEG_K = (512, 768)
_SEG_V = (768, 1024)
_SEG_QI = (1024, 1280)
_SEG_KIWI = (1280, 1408)
_SEG_FEAT = (1408, 3200)
_SEG_GATE = (3200, 5248)
_SEG_VAUG = (5248, 5760)
_PACKED_COLS = 5760


def _pack_w_in(w_in):
    pts = np.cumsum(SPLITS)[:-1].tolist()
    wq, wk, wv, wqi, wki, wwi, wfeat, wgate = jnp.split(w_in, pts, axis=-1)
    d = w_in.shape[0]
    pad = jnp.zeros((d, LANES - IDX_DIM - IDX_HEADS), w_in.dtype)
    wv_aug = jnp.concatenate([wv.reshape(d, KV_HEADS, HEAD_DIM), jnp.zeros((d, KV_HEADS, HEAD_DIM), w_in.dtype)],
                             axis=-1).reshape(d, 2 * KV_W)
    return jnp.concatenate([wq, wk, wv, wqi, wki, wwi, pad, wfeat, wgate, wv_aug], axis=-1).astype(BF16)


def _rope_lanes(y, cos, sin_signed, first_half):
    outs = []
    for c in range(y.shape[1] // LANES):
        xc = y[:, c * LANES:(c + 1) * LANES]
        partner = jnp.where(first_half, pltpu.roll(xc, 96, 1), pltpu.roll(xc, 32, 1))
        outs.append(xc * cos + partner * sin_signed)
    return outs[0] if len(outs) == 1 else jnp.concatenate(outs, axis=1)


def _in_proj_kernel(x_ref, g_ref, w_ref, inv_ref, q_ref, k_ref, v_ref, qi_ref, kiwi_ref, feat_ref, ga_ref, gb_ref,
                    *bf_refs, tm, period, offset):
    x = x_ref[...]
    h = _rms(x, g_ref[...]).astype(BF16)
    row = pl.program_id(0) * tm + lax.broadcasted_iota(jnp.int32, (tm, LANES), 0)
    pos = (row % period + offset).astype(F32)
    ang = pos * inv_ref[...]
    cos = jnp.cos(ang)
    sin = jnp.sin(ang)
    lane = lax.broadcasted_iota(jnp.int32, (tm, LANES), 1)
    first_half = (lane % HEAD_DIM) < (HEAD_DIM // 2)
    sin_signed = jnp.where(first_half, -sin, sin)

    def seg(s):
        return jnp.dot(h, w_ref[:, s[0]:s[1]], preferred_element_type=F32)

    q_ref[...] = _rope_lanes(seg(_SEG_Q), cos, sin_signed, first_half)
    k = _rope_lanes(seg(_SEG_K), cos, sin_signed, first_half)
    k_ref[...] = k
    v_ref[...] = seg(_SEG_V)
    qi_ref[...] = _rope_lanes(seg(_SEG_QI), cos, sin_signed, first_half)
    kiwi = seg(_SEG_KIWI)
    kiwi = jnp.where(lane < IDX_DIM, _rope_lanes(kiwi, cos, sin_signed, first_half), kiwi)
    kiwi_ref[...] = kiwi
    feat_ref[...] = seg(_SEG_FEAT)
    gates = jax.nn.sigmoid(seg(_SEG_GATE))
    ga_ref[...] = gates[:, :D_MODEL]
    gb_ref[...] = gates[:, D_MODEL:]
    if bf_refs:
        kbf_ref, kibf_ref, vaug_ref = bf_refs
        kbf_ref[...] = k.astype(BF16)
        kibf_ref[...] = kiwi[:, :IDX_DIM].astype(BF16)
        ones_half = jnp.where(lane >= HEAD_DIM, 1.0, 0.0)
        vaug_ref[...] = (seg(_SEG_VAUG) + jnp.concatenate([ones_half] * (2 * KV_W // LANES), axis=1)).astype(BF16)


def in_proj(x2d, norm_g, w_packed, inv_tiled, *, period, offset, attn_operands=False, tm=256):
    n = x2d.shape[0]
    assert n % tm == 0
    outs = [(w, F32) for w in (ATT_W, KV_W, KV_W, IDX_HEADS * IDX_DIM, LANES, SHIFT_W, D_MODEL, D_MODEL)]
    if attn_operands:
        outs += [(KV_W, BF16), (IDX_DIM, BF16), (2 * KV_W, BF16)]
    row = lambda w: pl.BlockSpec((tm, w), lambda i: (i, 0))
    full = lambda a: pl.BlockSpec(a.shape, lambda i: (0,) * a.ndim)
    return pl.pallas_call(
        functools.partial(_in_proj_kernel, tm=tm, period=period, offset=offset),
        grid=(n // tm,),
        in_specs=[row(D_MODEL), full(norm_g), full(w_packed), full(inv_tiled)],
        out_specs=[row(w) for w, _ in outs],
        out_shape=[jax.ShapeDtypeStruct((n, w), dt) for w, dt in outs],
        compiler_params=_cparams("parallel"),
        name="in_proj",
    )(x2d, norm_g, w_packed, inv_tiled)


def _mix_out_kernel(x_ref, oa_ref, ob_ref, ga_ref, gb_ref, pa_ref, pb_ref, wo_ref, o_ref):
    ya = jnp.dot(oa_ref[...].astype(BF16), pa_ref[...], preferred_element_type=F32)
    yb = jnp.dot(ob_ref[...].astype(BF16), pb_ref[...], preferred_element_type=F32)
    m = ga_ref[...] * ya + gb_ref[...] * yb
    o_ref[...] = x_ref[...] + jnp.dot(m.astype(BF16), wo_ref[...], preferred_element_type=F32)


def mix_out(x2d, oa, ob, ga, gb, pa, pb, wo, *, tm=256):
    n = x2d.shape[0]
    row = lambda w: pl.BlockSpec((tm, w), lambda i: (i, 0))
    full = lambda a: pl.BlockSpec(a.shape, lambda i: (0,) * a.ndim)
    return pl.pallas_call(
        _mix_out_kernel,
        grid=(n // tm,),
        in_specs=[row(D_MODEL), row(ATT_W), row(RWKV_W), row(D_MODEL), row(D_MODEL), full(pa), full(pb), full(wo)],
        out_specs=row(D_MODEL),
        out_shape=jax.ShapeDtypeStruct((n, D_MODEL), F32),
        compiler_params=_cparams("parallel"),
        name="mix_out",
    )(x2d, oa, ob, ga, gb, pa, pb, wo)


LOG2_E = math.log2(math.e)
INT_MIN = -(2 ** 31)
NEG_BIG = -1e30
IDX_SCALE = IDX_DIM ** -0.5 * IDX_HEADS ** -0.5


def _sortable(s):
    b = lax.bitcast_convert_type(s, jnp.int32)
    return b ^ ((b >> 31) & 0x7FFFFFFF)


def _lane_tile(x, reps):
    return x if reps == 1 else jnp.concatenate([x] * reps, axis=1)


def _select_threshold(keys_ref, nch, cw, rows, k_top):
    reps = cw // LANES

    def count_ge(thr_col):
        thr = jnp.broadcast_to(thr_col, (rows, LANES))

        def body(c, acc):
            kc = keys_ref[:, pl.ds(pl.multiple_of(c * cw, cw), cw)]
            for u in range(reps):
                acc = acc + jnp.where(kc[:, u * LANES:(u + 1) * LANES] >= thr, 1.0, 0.0)
            return acc
        acc = lax.fori_loop(0, nch, body, jnp.zeros((rows, LANES), F32))
        return jnp.sum(acc, axis=1, keepdims=True)

    def value_bit(i, t_u):
        cand = t_u | jnp.left_shift(jnp.int32(1), 31 - i)
        return jnp.where(count_ge(cand ^ INT_MIN) >= k_top, cand, t_u)

    t_u = lax.fori_loop(0, 32, value_bit, jnp.zeros((rows, 1), jnp.int32))
    tau = t_u ^ INT_MIN
    return tau, k_top - count_ge(tau + 1)


def _prefix_ones(n):
    return (jnp.arange(n)[:, None] <= jnp.arange(n)[None, :]).astype(BF16)


def _selection_bias(kc, tau_b, need_b, seen, tri):
    reps = kc.shape[1] // LANES
    eq = kc == _lane_tile(tau_b, reps)
    eq01 = jnp.where(eq, 1.0, 0.0)
    rank = jnp.dot(eq01.astype(BF16), tri, preferred_element_type=F32) + _lane_tile(seen, reps)
    tie_ok = jnp.where(eq, jnp.where(rank <= _lane_tile(need_b, reps), 0.0, NEG_BIG), NEG_BIG)
    bias = jnp.where(kc > _lane_tile(tau_b, reps), 0.0, tie_ok)
    return bias, seen + jnp.sum(eq01, axis=1, keepdims=True)


def _dsa_prompt_kernel(q_ref, qi_ref, kiwi_ref, k_ref, v_ref, ki_ref, tri_ref, o_ref,
                       keys_ref, wb_ref, m_ref, acc_ref, *, k_top, cw):
    tq = Q_BLOCK
    j = pl.program_id(1)
    nch = (j * tq + tq + cw - 1) // cw
    reps = cw // LANES
    t_b = j * tq + lax.broadcasted_iota(jnp.int32, (tq, LANES), 0)
    lane = lax.broadcasted_iota(jnp.int32, (tq, LANES), 1)

    qi = qi_ref[0]
    qi4 = jnp.concatenate([qi[:, h * IDX_DIM:(h + 1) * IDX_DIM] for h in range(IDX_HEADS)], axis=0).astype(BF16)
    wi = kiwi_ref[0][:, IDX_DIM:IDX_DIM + IDX_HEADS]
    for h in range(IDX_HEADS):
        wb_ref[h] = jnp.broadcast_to(wi[:, h:h + 1], (tq, LANES))

    def score_body(c, _):
        off = pl.multiple_of(c * cw, cw)
        kic = ki_ref[0, pl.ds(off, cw), :]
        dots = lax.dot_general(qi4, kic, (((1,), (1,)), ((), ())), preferred_element_type=F32)
        parts = []
        for u in range(reps):
            su = jnp.zeros((tq, LANES), F32)
            for h in range(IDX_HEADS):
                su = su + wb_ref[h] * jnp.maximum(dots[h * tq:(h + 1) * tq, u * LANES:(u + 1) * LANES], 0.0)
            kpos = lane + (off + u * LANES)
            parts.append(_sortable(jnp.where(kpos <= t_b, su * IDX_SCALE, -jnp.inf)))
        keys_ref[:, pl.ds(off, cw)] = _lane_tile(parts[0], 1) if reps == 1 else jnp.concatenate(parts, axis=1)
        return 0

    lax.fori_loop(0, nch, score_body, 0)

    tau, need = _select_threshold(keys_ref, nch, cw, tq, k_top)
    tau_b = jnp.broadcast_to(tau, (tq, LANES))
    need_b = jnp.broadcast_to(need, (tq, LANES))
    pos_b = _lane_tile(t_b - lane, reps) - jnp.concatenate(
        [jnp.full((tq, LANES), u * LANES, jnp.int32) for u in range(reps)], axis=1)

    q = q_ref[0] * (HEAD_DIM ** -0.5 * LOG2_E)
    qs = []
    for n in range(KV_HEADS):
        qs.append(jnp.concatenate([q[:, (2 * n) * HEAD_DIM:(2 * n + 1) * HEAD_DIM],
                                   q[:, (2 * n + 1) * HEAD_DIM:(2 * n + 2) * HEAD_DIM]], axis=0).astype(BF16))
    m_ref[...] = jnp.full(m_ref.shape, NEG_BIG, F32)
    acc_ref[...] = jnp.zeros(acc_ref.shape, F32)

    def attn_body(c, seen):
        off = pl.multiple_of(c * cw, cw)
        bias, seen = _selection_bias(keys_ref[:, pl.ds(off, cw)], tau_b, need_b, seen, tri_ref[...])
        bias = jnp.where(off <= pos_b, bias, NEG_BIG)
        bias2 = jnp.concatenate([bias, bias], axis=0)
        ss = []
        for n in range(KV_HEADS):
            kn = k_ref[0, pl.ds(off, cw), n * HEAD_DIM:(n + 1) * HEAD_DIM]
            ss.append(lax.dot_general(qs[n], kn, (((1,), (1,)), ((), ())), preferred_element_type=F32) + bias2)
        ps, alphas = [], []
        for n in range(KV_HEADS):
            m_prev = m_ref[n]
            m_new = jnp.maximum(m_prev, jnp.max(ss[n], axis=1, keepdims=True))
            ps.append(jnp.exp2(ss[n] - _lane_tile(m_new, reps)).astype(BF16))
            alphas.append(jnp.exp2(m_prev - m_new))
            m_ref[n] = m_new
        for n in range(KV_HEADS):
            vn = v_ref[0, pl.ds(off, cw), n * LANES:(n + 1) * LANES]
            acc_ref[n] = alphas[n] * acc_ref[n] + jnp.dot(ps[n], vn, preferred_element_type=F32)
        return seen

    lax.fori_loop(0, nch, attn_body, jnp.zeros((tq, LANES), F32))

    outs = []
    for n in range(KV_HEADS):
        acc = acc_ref[n]
        o = acc[:, :HEAD_DIM] / acc[:, HEAD_DIM:]
        outs += [o[:tq], o[tq:]]
    o_ref[0] = jnp.concatenate(outs, axis=1)


def dsa_prompt_attn(q, qi, kiwi, k_bf, v_aug, ki_bf, *, cw=512):
    B, S, _ = q.shape
    cw = min(cw, S)
    assert S % cw == 0 and cw % LANES == 0 and S % Q_BLOCK == 0
    k_top = min(TOPK_MAX, S // 4)
    blk = lambda w: pl.BlockSpec((1, Q_BLOCK, w), lambda b, j: (b, j, 0))
    res = lambda w: pl.BlockSpec((1, S, w), lambda b, j: (b, 0, 0), pipeline_mode=pl.Buffered(1))
    return pl.pallas_call(
        functools.partial(_dsa_prompt_kernel, k_top=k_top, cw=cw),
        grid=(B, S // Q_BLOCK),
        in_specs=[blk(ATT_W), blk(IDX_HEADS * IDX_DIM), blk(LANES), res(KV_W), res(2 * KV_W), res(IDX_DIM),
                  pl.BlockSpec((cw, cw), lambda b, j: (0, 0), pipeline_mode=pl.Buffered(1))],
        out_specs=blk(ATT_W),
        out_shape=jax.ShapeDtypeStruct((B, S, ATT_W), F32),
        scratch_shapes=[pltpu.VMEM((Q_BLOCK, S), jnp.int32),
                        pltpu.VMEM((IDX_HEADS, Q_BLOCK, LANES), F32),
                        pltpu.VMEM((KV_HEADS, 2 * Q_BLOCK, LANES), F32),
                        pltpu.VMEM((KV_HEADS, 2 * Q_BLOCK, LANES), F32)],
        compiler_params=_cparams("parallel", "arbitrary"),
        name="dsa_prompt",
    )(q, qi, kiwi, k_bf, v_aug, ki_bf, _prefix_ones(cw))


def _dsa_sample_kernel(pt_ref, q_ref, qi_ref, kiwi_ref, kn_ref, vn_ref, tri_ref, ck_hbm, cv_hbm, cik_hbm, o_ref,
                       kbuf2, vbuf2, ibuf2, sems, keys_ref, s_ref, *, n_pages, k_top, cw):
    b = pl.program_id(0)
    n_b = pl.num_programs(0)
    t_new = q_ref.shape[1]
    past = n_pages * PAGE_SIZE
    ppc = cw // PAGE_SIZE
    nch = n_pages // ppc
    reps = cw // LANES
    slot = b % 2

    streams = ((cik_hbm, ibuf2), (ck_hbm, kbuf2), (cv_hbm, vbuf2))

    def page_copy(which, seq, sl, p):
        src, dst = streams[which]
        return pltpu.make_async_copy(src.at[pt_ref[seq, p]], dst.at[sl, p], sems.at[sl, which])

    def start_pages(seq, sl):
        def body(p, _):
            for which in range(len(streams)):
                page_copy(which, seq, sl, p).start()
            return 0
        lax.fori_loop(0, n_pages, body, 0)

    def wait_pages(which):
        def body(p, _):
            page_copy(which, b, slot, p).wait()
            return 0
        lax.fori_loop(0, n_pages, body, 0)

    @pl.when(b == 0)
    def _():
        start_pages(b, slot)

    @pl.when(b + 1 < n_b)
    def _():
        start_pages(b + 1, 1 - slot)

    ibuf = ibuf2.at[slot]
    kbuf = kbuf2.at[slot]
    vbuf = vbuf2.at[slot]

    lane = lax.broadcasted_iota(jnp.int32, (t_new, LANES), 1)
    t_b = lax.broadcasted_iota(jnp.int32, (t_new, LANES), 0)
    qi = qi_ref[0]
    qi4 = jnp.concatenate([qi[:, h * IDX_DIM:(h + 1) * IDX_DIM] for h in range(IDX_HEADS)], axis=0).astype(BF16)
    kiwi = kiwi_ref[0]
    wi = kiwi[:, IDX_DIM:IDX_DIM + IDX_HEADS]
    wb = [jnp.broadcast_to(wi[:, h:h + 1], (t_new, LANES)) for h in range(IDX_HEADS)]

    def combine(dots_u):
        su = jnp.zeros((t_new, LANES), F32)
        for h in range(IDX_HEADS):
            su = su + wb[h] * jnp.maximum(dots_u[h * t_new:(h + 1) * t_new], 0.0)
        return su * IDX_SCALE

    wait_pages(0)

    def score_body(c, _):
        kic = ibuf[pl.ds(c * ppc, ppc)].reshape(cw, IDX_DIM).astype(BF16)
        dots = lax.dot_general(qi4, kic, _NT, preferred_element_type=F32)
        parts = [_sortable(combine(dots[:, u * LANES:(u + 1) * LANES])) for u in range(reps)]
        keys_ref[:, pl.ds(pl.multiple_of(c * cw, cw), cw)] = jnp.concatenate(parts, axis=1)
        return 0

    lax.fori_loop(0, nch, score_body, 0)
    ki_new = jnp.concatenate([kiwi[:, :IDX_DIM], jnp.zeros((LANES - t_new, IDX_DIM), F32)], axis=0).astype(BF16)
    dots_new = lax.dot_general(qi4, ki_new, _NT, preferred_element_type=F32)
    s_new = jnp.where(lane <= t_b, combine(dots_new), -jnp.inf)
    neg = jnp.full((t_new, LANES), -jnp.inf, F32)
    keys_ref[:, pl.ds(past, cw)] = _sortable(jnp.concatenate([s_new] + [neg] * (reps - 1), axis=1))

    tau, need = _select_threshold(keys_ref, nch + 1, cw, t_new, k_top)
    tau_b = jnp.broadcast_to(tau, (t_new, LANES))
    need_b = jnp.broadcast_to(need, (t_new, LANES))

    q = q_ref[0] * (HEAD_DIM ** -0.5 * LOG2_E)
    qs = []
    for n in range(KV_HEADS):
        qs.append(jnp.concatenate([q[:, (2 * n) * HEAD_DIM:(2 * n + 1) * HEAD_DIM],
                                   q[:, (2 * n + 1) * HEAD_DIM:(2 * n + 2) * HEAD_DIM]], axis=0).astype(BF16))
    rows2 = 2 * t_new

    def logits(kc_bf, bias, off, mx):
        width = bias.shape[1]
        bias2 = jnp.concatenate([bias, bias], axis=0)
        new_mx = []
        for n in range(KV_HEADS):
            s = lax.dot_general(qs[n], kc_bf[:, n * HEAD_DIM:(n + 1) * HEAD_DIM], _NT, preferred_element_type=F32) + bias2
            s_ref[n, :, pl.ds(off, width)] = s
            m = mx[n]
            for u in range(width // LANES):
                m = jnp.maximum(m, s[:, u * LANES:(u + 1) * LANES])
            new_mx.append(m)
        return tuple(new_mx)

    wait_pages(1)

    def logit_body(c, carry):
        mx, seen = carry
        off = pl.multiple_of(c * cw, cw)
        kc = kbuf[pl.ds(c * ppc, ppc)].reshape(cw, KV_W).astype(BF16)
        bias, seen = _selection_bias(keys_ref[:, pl.ds(off, cw)], tau_b, need_b, seen, tri_ref[...])
        return logits(kc, bias, off, mx), seen

    mx, seen = lax.fori_loop(0, nch, logit_body,
                             (tuple(jnp.full((rows2, LANES), NEG_BIG, F32) for _ in range(KV_HEADS)),
                              jnp.zeros((t_new, LANES), F32)))
    pad_rows = jnp.zeros((LANES - t_new, KV_W), F32)
    kn_pad = jnp.concatenate([kn_ref[0], pad_rows], axis=0).astype(BF16)
    vn_pad = jnp.concatenate([vn_ref[0], pad_rows], axis=0).astype(BF16)
    bias_new, _ = _selection_bias(keys_ref[:, pl.ds(past, LANES)], tau_b, need_b, seen, tri_ref[:LANES, :LANES])
    mx = logits(kn_pad, jnp.where(lane <= t_b, bias_new, NEG_BIG), past, mx)
    m_row = [jnp.broadcast_to(jnp.max(m, axis=1, keepdims=True), (rows2, LANES)) for m in mx]

    def weighted(vc_bf, off, width, carry):
        accs, sums = carry
        new_accs, new_sums = [], []
        for n in range(KV_HEADS):
            p = jnp.exp2(s_ref[n, :, pl.ds(off, width)] - _lane_tile(m_row[n], width // LANES))
            l = sums[n]
            for u in range(width // LANES):
                l = l + p[:, u * LANES:(u + 1) * LANES]
            new_sums.append(l)
            new_accs.append(accs[n] + jnp.dot(p.astype(BF16), vc_bf[:, n * HEAD_DIM:(n + 1) * HEAD_DIM],
                                              preferred_element_type=F32))
        return tuple(new_accs), tuple(new_sums)

    wait_pages(2)

    def pv_body(c, carry):
        vc = vbuf[pl.ds(c * ppc, ppc)].reshape(cw, KV_W).astype(BF16)
        return weighted(vc, pl.multiple_of(c * cw, cw), cw, carry)

    zero = lambda w: tuple(jnp.zeros((rows2, w), F32) for _ in range(KV_HEADS))
    carry = lax.fori_loop(0, nch, pv_body, (zero(HEAD_DIM), zero(LANES)))
    accs, sums = weighted(vn_pad, past, LANES, carry)

    outs = []
    for n in range(KV_HEADS):
        o = accs[n] / jnp.sum(sums[n], axis=1, keepdims=True)
        outs += [o[:t_new], o[t_new:]]
    o_ref[0] = jnp.concatenate(outs, axis=1)


def dsa_sample_attn(q, qi, kiwi, k_new, v_new, cache_k, cache_v, cache_idx_k, page_table, *, cw=512):
    Bd, T, _ = q.shape
    n_pages = page_table.shape[1]
    past = n_pages * PAGE_SIZE
    cw = min(cw, past)
    assert cw % PAGE_SIZE == 0 and past % cw == 0 and T <= LANES
    k_top = min(TOPK_MAX, (past + T) // 4)
    n_phys = cache_k.shape[0]
    blk = lambda w: pl.BlockSpec((1, T, w), lambda b, pt: (b, 0, 0))
    hbm = pl.BlockSpec(memory_space=pl.ANY)
    gs = pltpu.PrefetchScalarGridSpec(
        num_scalar_prefetch=1,
        grid=(Bd,),
        in_specs=[blk(ATT_W), blk(IDX_HEADS * IDX_DIM), blk(LANES), blk(KV_W), blk(KV_W),
                  pl.BlockSpec((cw, cw), lambda b, pt: (0, 0)), hbm, hbm, hbm],
        out_specs=blk(ATT_W),
        scratch_shapes=[pltpu.VMEM((2, n_pages, PAGE_SIZE, KV_W), F32),
                        pltpu.VMEM((2, n_pages, PAGE_SIZE, KV_W), F32),
                        pltpu.VMEM((2, n_pages, PAGE_SIZE, IDX_DIM), F32),
                        pltpu.SemaphoreType.DMA((2, 3)),
                        pltpu.VMEM((T, past + cw), jnp.int32),
                        pltpu.VMEM((KV_HEADS, 2 * T, past + LANES), F32)])
    return pl.pallas_call(
        functools.partial(_dsa_sample_kernel, n_pages=n_pages, k_top=k_top, cw=cw),
        grid_spec=gs,
        out_shape=jax.ShapeDtypeStruct((Bd, T, ATT_W), F32),
        compiler_params=_cparams("arbitrary"),
        name="dsa_sample",
    )(page_table, q, qi, kiwi, k_new, v_new, _prefix_ones(cw),
      cache_k.reshape(n_phys, PAGE_SIZE, KV_W), cache_v.reshape(n_phys, PAGE_SIZE, KV_W), cache_idx_k)


def _cross_attn_kernel(x_ref, g_ref, wq_ref, mk_ref, mv_ref, wo_ref, o_ref):
    x = x_ref[0]
    q = jnp.dot(_rms(x, g_ref[...]).astype(BF16), wq_ref[...], preferred_element_type=F32) * (MEM_HD ** -0.5)
    q = q.astype(BF16)
    mk = mk_ref[0].astype(BF16)
    mv = mv_ref[0].astype(BF16)
    outs = []
    for h in range(MEM_HEADS):
        sl = slice(h * MEM_HD, (h + 1) * MEM_HD)
        s = lax.dot_general(q[:, sl], mk[:, sl], _NT, preferred_element_type=F32)
        p = jnp.exp(s - jnp.max(s, axis=1, keepdims=True))
        p = p / jnp.sum(p, axis=1, keepdims=True)
        outs.append(jnp.dot(p.astype(BF16), mv[:, sl], preferred_element_type=F32))
    o = jnp.concatenate(outs, axis=1).astype(BF16)
    o_ref[0] = x + jnp.dot(o, wo_ref[...], preferred_element_type=F32)


def cross_attn_mem(x, norm_g, wq_bf, mk, mv, wo_bf, *, tm=512):
    B, T, _ = x.shape
    tm = min(tm, T)
    assert T % tm == 0
    M = mk.shape[1]
    full = lambda a: pl.BlockSpec(a.shape, lambda b, i: (0,) * a.ndim)
    mem = pl.BlockSpec((1, M, MEM_W), lambda b, i: (b, 0, 0))
    return pl.pallas_call(
        _cross_attn_kernel,
        grid=(B, T // tm),
        in_specs=[pl.BlockSpec((1, tm, D_MODEL), lambda b, i: (b, i, 0)), full(norm_g), full(wq_bf), mem, mem, full(wo_bf)],
        out_specs=pl.BlockSpec((1, tm, D_MODEL), lambda b, i: (b, i, 0)),
        out_shape=jax.ShapeDtypeStruct(x.shape, F32),
        compiler_params=_cparams("parallel", "arbitrary"),
        name="cross_attn",
    )(x, norm_g, wq_bf, mk, mv, wo_bf)


def _norm_proj_kernel(x_ref, g_ref, w_ref, o_ref):
    o_ref[...] = jnp.dot(_rms(x_ref[...], g_ref[...]).astype(BF16), w_ref[...], preferred_element_type=F32)


def norm_proj(x2d, norm_g, w_bf, *, tm=256):
    n = x2d.shape[0]
    tm = min(tm, n)
    assert n % tm == 0
    return pl.pallas_call(
        _norm_proj_kernel,
        grid=(n // tm,),
        in_specs=[pl.BlockSpec((tm, x2d.shape[1]), lambda i: (i, 0)), pl.BlockSpec(norm_g.shape, lambda i: (0, 0)),
                  pl.BlockSpec(w_bf.shape, lambda i: (0, 0))],
        out_specs=pl.BlockSpec((tm, w_bf.shape[1]), lambda i: (i, 0)),
        out_shape=jax.ShapeDtypeStruct((n, w_bf.shape[1]), F32),
        compiler_params=_cparams("parallel"),
        name="norm_proj",
    )(x2d, norm_g, w_bf)


PEER_HALF = PEER_DK // 2
PEER_SUB = 256
PEER_TE = 2 * PEER_SUB
PEER_CAND_CAP = tuple(PEER_TOPK // (a + 1) for a in range(PEER_TOPK))
PEER_CAND_ROWS = -(-sum(PEER_CAND_CAP) // 8) * 8


def _extract_top(work_ref, n_keys, tn, on_pick):
    sub = lax.broadcasted_iota(jnp.int32, (n_keys, tn), 0).astype(F32)
    for a in range(PEER_TOPK):
        w = work_ref[...]
        m = jnp.max(w, axis=0, keepdims=True)
        idx = jnp.min(jnp.where(w == m, sub, float(n_keys)), axis=0, keepdims=True)
        hit = sub == idx
        work_ref[...] = jnp.where(hit, -jnp.inf, w)
        on_pick(a, m, hit)


def _peer_kernel(x_ref, g_ref, gf_ref, wq_ref, k1_ref, k2_ref, u_ref, vt_ref, o_ref,
                 xn_ref, s_ref, rank_ref, vals_ref, work_ref, cand_ref, c_ref, l_ref, e2_ref, r2_ref, acc_ref, ht_ref, w_ref,
                 *, tn, final_norm):
    e = pl.program_id(1)
    n_e = pl.num_programs(1)

    @pl.when(e == 0)
    def _route():
        xn = _rms(x_ref[...], g_ref[...]).astype(BF16)
        xn_ref[...] = xn
        q = jnp.dot(xn, wq_ref[...], preferred_element_type=F32).astype(BF16)
        for h in range(PEER_HEADS):
            q1 = q[:, h * PEER_DK:h * PEER_DK + PEER_HALF]
            q2 = q[:, h * PEER_DK + PEER_HALF:(h + 1) * PEER_DK]
            nt = (((1,), (1,)), ((), ()))
            s_ref[h] = lax.dot_general(k1_ref[h], q1, nt, preferred_element_type=F32)
            s_ref[PEER_HEADS + h] = lax.dot_general(k2_ref[h], q2, nt, preferred_element_type=F32)

        def top_keys(hh, _):
            work_ref[...] = s_ref[hh]
            rank_ref[hh] = jnp.full((N_KEYS, tn), float(PEER_TOPK), F32)

            def pick(a, m, hit):
                vals_ref[hh, a:a + 1, :] = m
                rank_ref[hh] = jnp.where(hit, float(a), rank_ref[hh])
            _extract_top(work_ref, N_KEYS, tn, pick)
            return 0

        lax.fori_loop(0, 2 * PEER_HEADS, top_keys, 0)

        def gates(h, _):
            v1 = vals_ref[h]
            v2 = vals_ref[PEER_HEADS + h]
            rows = [v1[a:a + 1, :] + v2[0:cap, :] for a, cap in enumerate(PEER_CAND_CAP)]
            rows.append(jnp.full((PEER_CAND_ROWS - sum(PEER_CAND_CAP), tn), -jnp.inf, F32))
            cand_ref[...] = jnp.concatenate(rows, axis=0)
            _extract_top(cand_ref, PEER_CAND_ROWS, tn, lambda a, m, hit: None)
            picked = jnp.where(cand_ref[...] == -jnp.inf, 1.0, 0.0)
            e1 = jnp.exp(v1 - v1[0:1, :])
            e2 = jnp.exp(v2 - v2[0:1, :])
            z = jnp.zeros((1, tn), F32)
            r1 = rank_ref[h]
            lfull = jnp.zeros((N_KEYS, tn), F32)
            off = 0
            for a, cap in enumerate(PEER_CAND_CAP):
                sel = picked[off:off + cap, :]
                off += cap
                z = z + e1[a:a + 1, :] * jnp.sum(sel * e2[0:cap, :], axis=0, keepdims=True)
                lfull = lfull + jnp.where(r1 == float(a), jnp.sum(sel, axis=0, keepdims=True), 0.0)
            l_ref[h] = lfull
            c_ref[h] = jnp.where(r1 < float(PEER_TOPK), jnp.exp(s_ref[h] - v1[0:1, :]) / z, 0.0)
            e2_ref[h] = jnp.exp(s_ref[PEER_HEADS + h] - v2[0:1, :]).astype(BF16)
            r2_ref[h] = rank_ref[PEER_HEADS + h].astype(BF16)
            return 0

        lax.fori_loop(0, PEER_HEADS, gates, 0)
        acc_ref[...] = jnp.zeros(acc_ref.shape, F32)

        ht_ref[...] = jnp.zeros(ht_ref.shape, F32)
        w_ref[...] = jnp.zeros(w_ref.shape, BF16)

    n_sub = 2 * (n_e - 1)
    keys_per_sub = PEER_SUB // N_KEYS

    def gated(ht, sub):
        live = jnp.where(jnp.logical_and(sub >= 0, sub < n_sub), 1.0, 0.0)
        base = jnp.clip(sub, 0, n_sub - 1) * keys_per_sub
        parts = []
        zero = jnp.zeros((), BF16)
        for ii in range(keys_per_sub):
            g = jnp.zeros((N_KEYS, tn), BF16)
            for h in range(PEER_HEADS):
                cb = jnp.broadcast_to((c_ref[h, pl.ds(base + ii, 1), :] * live).astype(BF16), (N_KEYS, tn))
                lb = jnp.broadcast_to(l_ref[h, pl.ds(base + ii, 1), :].astype(BF16), (N_KEYS, tn))
                g = g + cb * jnp.where(r2_ref[h] < lb, e2_ref[h], zero)
            hi = ht[ii * N_KEYS:(ii + 1) * N_KEYS]
            act = 0.5 * hi * (1.0 + lax.erf(hi * (2.0 ** -0.5)))
            parts.append(g * act.astype(BF16))
        return parts[0] if len(parts) == 1 else jnp.concatenate(parts, axis=0)

    nt = (((1,), (1,)), ((), ()))
    acc_ref[...] += jnp.dot(vt_ref[:, :PEER_SUB], w_ref[...], preferred_element_type=F32)
    ht_new = lax.dot_general(u_ref[:PEER_SUB, :], xn_ref[...], nt, preferred_element_type=F32)
    w_mid = gated(ht_ref[...], 2 * e - 1)
    ht_ref[...] = lax.dot_general(u_ref[PEER_SUB:, :], xn_ref[...], nt, preferred_element_type=F32)
    w_ref[...] = gated(ht_new, 2 * e)
    acc_ref[...] += jnp.dot(vt_ref[:, PEER_SUB:], w_mid, preferred_element_type=F32)

    @pl.when(e == n_e - 1)
    def _finish():
        y = x_ref[...] + acc_ref[...].T
        if final_norm:
            y = _rms(y, gf_ref[...])
        o_ref[...] = y


def peer_ffn(x2d, norm_g, final_g, wq_bf, k1_bf, k2_bf, u_bf, vt_bf, *, final_norm, tn=512):
    n = x2d.shape[0]
    tn = min(tn, n)
    assert n % tn == 0 and tn % LANES == 0
    n_blk = u_bf.shape[0] // PEER_TE
    full = lambda a: pl.BlockSpec(a.shape, lambda t, e: (0,) * a.ndim)
    return pl.pallas_call(
        functools.partial(_peer_kernel, tn=tn, final_norm=final_norm),
        grid=(n // tn, n_blk + 1),
        in_specs=[pl.BlockSpec((tn, D_MODEL), lambda t, e: (t, 0)), full(norm_g), full(final_g), full(wq_bf),
                  full(k1_bf), full(k2_bf),
                  pl.BlockSpec((PEER_TE, D_MODEL), lambda t, e: (jnp.minimum(e, n_blk - 1), 0)),
                  pl.BlockSpec((D_MODEL, PEER_TE), lambda t, e: (0, jnp.maximum(e - 1, 0)))],
        out_specs=pl.BlockSpec((tn, D_MODEL), lambda t, e: (t, 0)),
        out_shape=jax.ShapeDtypeStruct((n, D_MODEL), F32),
        scratch_shapes=[pltpu.VMEM((tn, D_MODEL), BF16),
                        pltpu.VMEM((2 * PEER_HEADS, N_KEYS, tn), F32),
                        pltpu.VMEM((2 * PEER_HEADS, N_KEYS, tn), F32),
                        pltpu.VMEM((2 * PEER_HEADS, PEER_TOPK, tn), F32),
                        pltpu.VMEM((N_KEYS, tn), F32),
                        pltpu.VMEM((PEER_CAND_ROWS, tn), F32),
                        pltpu.VMEM((PEER_HEADS, N_KEYS, tn), F32),
                        pltpu.VMEM((PEER_HEADS, N_KEYS, tn), F32),
                        pltpu.VMEM((PEER_HEADS, N_KEYS, tn), BF16),
                        pltpu.VMEM((PEER_HEADS, N_KEYS, tn), BF16),
                        pltpu.VMEM((D_MODEL, tn), F32),
                        pltpu.VMEM((PEER_SUB, tn), F32),
                        pltpu.VMEM((PEER_SUB, tn), BF16)],
        compiler_params=_cparams("parallel", "arbitrary"),
        name="peer_ffn",
    )(x2d, norm_g, final_g, wq_bf, k1_bf, k2_bf, u_bf, vt_bf)


RWKV_GROUP = 4
RWKV_GW = RWKV_GROUP * RWKV_N
RWKV_CHUNK = 64


def _split_bf16(x, terms):
    parts = []
    for _ in range(terms):
        p = x.astype(BF16)
        parts.append(p)
        x = x - p.astype(F32)
    return parts


def _dot3(a, b, dims=(((1,), (0,)), ((), ()))):
    ah, al = _split_bf16(a, 2)
    bh, bl = _split_bf16(b, 2)
    d = lambda x, y: lax.dot_general(x, y, dims, preferred_element_type=F32)
    return d(ah, bh) + d(al, bh) + d(ah, bl)


def _dot1(a, b, dims=(((1,), (0,)), ((), ()))):
    return lax.dot_general(a.astype(BF16), b.astype(BF16), dims, preferred_element_type=F32)


def _dot_exact_rhs(a, b_exact_bf16):
    return sum(jnp.dot(p, b_exact_bf16, preferred_element_type=F32) for p in _split_bf16(a, 3))


def _dot_exact_lhs(a_exact_bf16, b):
    return sum(jnp.dot(a_exact_bf16, p, preferred_element_type=F32) for p in _split_bf16(b, 3))


_NT = (((1,), (1,)), ((), ()))
_TN = (((0,), (0,)), ((), ()))


def _rwkv_kernel(feat_ref, prev_ref, s0t_ref, mu_ref, w0_ref, w2_ref, a0_ref, a2_ref, g2_ref, kk_ref, ka_ref, rk_ref,
                 lng_ref, lnb_ref, hsum_ref, tri_ref, o_ref, sfin_ref, state_ref, carry_ref, *, c_len, n_double):
    c = pl.program_id(1)
    n_c = pl.num_programs(1)
    n4 = RWKV_GROUP * c_len

    @pl.when(c == 0)
    def _init():
        carry_ref[0:1, :] = prev_ref[0]
        state_ref[...] = jnp.zeros(state_ref.shape, F32)
        for h in range(RWKV_HEADS):
            g, hh = divmod(h, RWKV_GROUP)
            state_ref[g, hh * RWKV_N:(hh + 1) * RWKV_N, hh * RWKV_N:(hh + 1) * RWKV_N] = s0t_ref[0, h]

    feat = feat_ref[0]
    row = lax.broadcasted_iota(jnp.int32, feat.shape, 0)
    shifted = jnp.where(row == 0, jnp.broadcast_to(carry_ref[0:1, :], feat.shape), pltpu.roll(feat, 1, 0))
    carry_ref[0:1, :] = feat[c_len - 1:c_len, :]
    mixed = feat + (shifted - feat) * mu_ref[...]
    r = mixed[:, 0:RWKV_W]
    k = mixed[:, RWKV_W:2 * RWKV_W]
    v = mixed[:, 2 * RWKV_W:3 * RWKV_W]
    o1 = 3 * RWKV_W
    wd = mixed[:, o1:o1 + LORA_W]
    ad = mixed[:, o1 + LORA_W:o1 + LORA_W + LORA_A]
    gd = mixed[:, o1 + LORA_W + LORA_A:]

    zw = -(w0_ref[...] + _dot3(jnp.tanh(wd), w2_ref[...]))
    log_w = -(jnp.maximum(zw, 0.0) + jnp.log1p(jnp.exp(-jnp.abs(zw)))) - 0.5
    dlog = -jnp.exp(log_w)
    a = jax.nn.sigmoid(a0_ref[...] + _dot3(ad, a2_ref[...]))
    gate = _dot3(jax.nn.sigmoid(gd), g2_ref[...])
    hsum = hsum_ref[...]
    kk = k * kk_ref[...]
    kk = kk / jnp.maximum(jnp.sqrt(_dot_exact_rhs(kk * kk, hsum)), 1e-12)
    k_eff = k * (1.0 + (a - 1.0) * ka_ref[...])

    cum = _dot_exact_lhs(tri_ref[...], dlog)
    g_in = jnp.exp(cum)
    g_inv = jnp.exp(-cum)
    kk_t = kk * jnp.exp(cum - dlog)
    b_h = kk * a * g_inv
    k_h = k_eff * g_inv
    r_t = r * g_in
    g_last = g_in[c_len - 1:c_len, :]

    rr = lax.broadcasted_iota(jnp.int32, (n4, n4), 0)
    cc = lax.broadcasted_iota(jnp.int32, (n4, n4), 1)
    eye = jnp.where(rr == cc, 1.0, 0.0)
    lane_g = lax.broadcasted_iota(jnp.int32, (c_len, RWKV_GW), 1) // RWKV_N
    rr_g = lax.broadcasted_iota(jnp.int32, (RWKV_GW, RWKV_GW), 0)
    cc_g = lax.broadcasted_iota(jnp.int32, (RWKV_GW, RWKV_GW), 1)

    groups = range(RWKV_HEADS // RWKV_GROUP)
    each = lambda f: [f(g) for g in groups]

    def stack(x, g):
        xg = x[:, g * RWKV_GW:(g + 1) * RWKV_GW]
        return jnp.concatenate([jnp.where(lane_g == h, xg, 0.0) for h in range(RWKV_GROUP)], axis=0)

    kks, rs, bs = each(lambda g: stack(kk_t, g)), each(lambda g: stack(r_t, g)), each(lambda g: stack(b_h, g))
    ks, vs = each(lambda g: stack(k_h, g)), each(lambda g: stack(v, g))
    l_b = each(lambda g: jnp.where(rr > cc, _dot3(kks[g], bs[g], _NT), 0.0))
    l_k = each(lambda g: jnp.where(rr > cc, _dot3(kks[g], ks[g], _NT), 0.0))
    m_b = each(lambda g: jnp.where(rr >= cc, _dot1(rs[g], bs[g], _NT), 0.0))
    m_k = each(lambda g: jnp.where(rr >= cc, _dot1(rs[g], ks[g], _NT), 0.0))
    npow = each(lambda g: -l_b[g])
    t_inv = each(lambda g: eye + npow[g])
    for _ in range(n_double):
        npow = each(lambda g: _dot3(npow[g], npow[g]))
        t_inv = each(lambda g: t_inv[g] + _dot3(t_inv[g], npow[g]))
    w1 = each(lambda g: -_dot3(t_inv[g], kks[g]))
    lkv = each(lambda g: _dot3(l_k[g], vs[g]))
    u1 = each(lambda g: -_dot3(t_inv[g], lkv[g]))
    gl = each(lambda g: g_last[:, g * RWKV_GW:(g + 1) * RWKV_GW])
    bsg = each(lambda g: bs[g] * gl[g])
    ksg = each(lambda g: ks[g] * gl[g])
    g_mat = each(lambda g: jnp.where(rr_g == cc_g, jnp.broadcast_to(gl[g], (RWKV_GW, RWKV_GW)), 0.0)
                 + _dot3(bsg[g], w1[g], _TN))
    h_mat = each(lambda g: _dot3(bsg[g], u1[g], _TN) + _dot3(ksg[g], vs[g], _TN))
    r_y = each(lambda g: rs[g] + _dot1(m_b[g], w1[g]))
    y_0 = each(lambda g: _dot1(m_b[g], u1[g]) + _dot1(m_k[g], vs[g]))
    s_t = each(lambda g: state_ref[g])
    y_st = each(lambda g: _dot1(r_y[g], s_t[g]) + y_0[g])
    s_new = each(lambda g: _dot3(g_mat[g], s_t[g]) + h_mat[g])
    for g in groups:
        state_ref[g] = s_new[g]
    ys = each(lambda g: sum(y_st[g][h * c_len:(h + 1) * c_len] for h in range(RWKV_GROUP)))
    y = jnp.concatenate(ys, axis=1)

    inv_n = 1.0 / RWKV_N
    mean = _dot_exact_rhs(y, hsum) * inv_n
    yc = y - mean
    var = _dot_exact_rhs(yc * yc, hsum) * inv_n
    yn = yc * lax.rsqrt(var + LN_X_EPS) * lng_ref[...] + lnb_ref[...]
    bonus = _dot_exact_rhs(r * k_eff * rk_ref[...], hsum) * v
    o_ref[0] = (yn + bonus) * gate

    @pl.when(c == n_c - 1)
    def _fin():
        for h in range(RWKV_HEADS):
            g, hh = divmod(h, RWKV_GROUP)
            sfin_ref[0, h] = state_ref[g, hh * RWKV_N:(hh + 1) * RWKV_N, hh * RWKV_N:(hh + 1) * RWKV_N]


def rwkv_mix(feat, prev, s0, lw, *, c_len):
    B, T, _ = feat.shape
    assert T % c_len == 0 and c_len % 8 == 0
    n_double = max(0, (c_len - 1).bit_length() - 1)
    hsum = (jnp.arange(RWKV_W)[:, None] // RWKV_N == jnp.arange(RWKV_W)[None, :] // RWKV_N).astype(BF16)
    tri = (jnp.arange(c_len)[:, None] >= jnp.arange(c_len)[None, :]).astype(BF16)
    row = lambda a: a.reshape(1, -1).astype(F32)
    params = [row(lw['rwkv_mu']), row(lw['rwkv_w0']), lw['rwkv_w2'], row(lw['rwkv_a0']), lw['rwkv_a2'], lw['rwkv_g2'],
              row(lw['rwkv_k_k']), row(lw['rwkv_k_a']), row(lw['rwkv_r_k']), row(lw['rwkv_ln_g']), row(lw['rwkv_ln_b']),
              hsum, tri]
    full = lambda a: pl.BlockSpec(a.shape, lambda b, c: (0,) * a.ndim)
    st_spec = pl.BlockSpec((1, RWKV_HEADS, RWKV_N, RWKV_N), lambda b, c: (b, 0, 0, 0))
    out, s_fin_t = pl.pallas_call(
        functools.partial(_rwkv_kernel, c_len=c_len, n_double=n_double),
        grid=(B, T // c_len),
        in_specs=[pl.BlockSpec((1, c_len, SHIFT_W), lambda b, c: (b, c, 0)),
                  pl.BlockSpec((1, 1, SHIFT_W), lambda b, c: (b, 0, 0)), st_spec] + [full(p) for p in params],
        out_specs=[pl.BlockSpec((1, c_len, RWKV_W), lambda b, c: (b, c, 0)), st_spec],
        out_shape=[jax.ShapeDtypeStruct((B, T, RWKV_W), F32),
                   jax.ShapeDtypeStruct((B, RWKV_HEADS, RWKV_N, RWKV_N), F32)],
        scratch_shapes=[pltpu.VMEM((RWKV_HEADS // RWKV_GROUP, RWKV_GW, RWKV_GW), F32),
                        pltpu.VMEM((8, SHIFT_W), F32)],
        compiler_params=_cparams("parallel", "arbitrary"),
        name="rwkv_mix",
    )(feat, prev[:, None, :], jnp.swapaxes(s0, -1, -2), *params)
    return out, jnp.swapaxes(s_fin_t, -1, -2)


def _rope_inv_tiled():
    half = HEAD_DIM // 2
    inv = ROPE_THETA ** (-jnp.arange(half, dtype=jnp.float32) / half)
    return jnp.tile(inv, LANES // half)[None, :]


def trunk_layer(x, period, offset, lw, paged, shift_prev, wkv0, mk, mv, final_g, is_last):
    B, T, _ = x.shape
    n = B * T
    x2d = x.reshape(n, D_MODEL)
    proj = in_proj(x2d, lw['norm_mix'], lw['w_in_packed'], _rope_inv_tiled(), period=period, offset=offset,
                   attn_operands=paged is None)
    q, k, v, qi, kiwi, feat, ga, gb = proj[:8]
    b3 = lambda a: a.reshape(B, T, a.shape[-1])
    ki = kiwi[:, :IDX_DIM].reshape(B, T, IDX_DIM)
    feat = b3(feat)
    if paged is None:
        k_bf, ki_bf, v_aug = proj[8:]
        o_a = dsa_prompt_attn(b3(q), b3(qi), b3(kiwi), b3(k_bf), b3(v_aug), b3(ki_bf))
    else:
        o_a = dsa_sample_attn(b3(q), b3(qi), b3(kiwi), b3(k), b3(v), *paged)
    o_b, wkv_fin = rwkv_mix(feat, shift_prev, wkv0, lw, c_len=min(T, RWKV_CHUNK))
    shift_last = feat[:, -1]
    x2d = mix_out(x2d, o_a.reshape(n, ATT_W), o_b.reshape(n, RWKV_W), ga, gb, lw['proj_a'], lw['proj_b'], lw['w_out'])
    x = cross_attn_mem(x2d.reshape(B, T, D_MODEL), lw['norm_ca'], lw['w_cq'], mk, mv, lw['w_co'])
    x2d = peer_ffn(x.reshape(n, D_MODEL), lw['norm_ffn'], final_g, lw['peer_wq'], lw['peer_k1'],
                   lw['peer_k2'], lw['peer_u'], lw['peer_vt'], final_norm=is_last)
    new_k = k.reshape(B, T, KV_HEADS, HEAD_DIM)
    new_v = v.reshape(B, T, KV_HEADS, HEAD_DIM)
    return x2d.reshape(B, T, D_MODEL), (new_k, new_v, ki, shift_last, wkv_fin)


def kernel(x_prompt, x_sample, cache_k, cache_v, cache_idx_k, state_shift, state_wkv, cache_mem_k, cache_mem_v, page_table, mem_prompt, norm_mix, w_in, rwkv_mu, rwkv_w0, rwkv_w2, rwkv_a0, rwkv_a2, rwkv_g2, rwkv_k_k, rwkv_k_a, rwkv_r_k, rwkv_ln_g, rwkv_ln_b, proj_a, proj_b, w_out, norm_ca, norm_mem, w_cq, w_mk, w_mv, w_co, norm_ffn, peer_wq, peer_k1, peer_k2, peer_u, peer_v, norm_final):
    B, S, _ = x_prompt.shape
    Bd, T, _ = x_sample.shape
    depth = w_in.shape[0]
    past = page_table.shape[1] * PAGE_SIZE
    xp, xs = x_prompt, x_sample
    st_p_all, st_s_all, mem_all = [], [], []
    bf = lambda a: a.astype(BF16)
    row = lambda a: a.reshape(1, -1)
    final_g = row(norm_final)
    M = mem_prompt.shape[1]
    for l in range(depth):
        lw = {
            'norm_mix': row(norm_mix[l]), 'w_in_packed': _pack_w_in(w_in[l]), 'rwkv_mu': rwkv_mu[l], 'rwkv_w0': rwkv_w0[l],
            'rwkv_w2': rwkv_w2[l], 'rwkv_a0': rwkv_a0[l], 'rwkv_a2': rwkv_a2[l], 'rwkv_g2': rwkv_g2[l],
            'rwkv_k_k': rwkv_k_k[l], 'rwkv_k_a': rwkv_k_a[l], 'rwkv_r_k': rwkv_r_k[l], 'rwkv_ln_g': rwkv_ln_g[l],
            'rwkv_ln_b': rwkv_ln_b[l], 'proj_a': bf(proj_a[l]), 'proj_b': bf(proj_b[l]), 'w_out': bf(w_out[l]),
            'norm_ca': row(norm_ca[l]), 'w_cq': bf(w_cq[l]), 'w_co': bf(w_co[l]), 'norm_ffn': row(norm_ffn[l]),
            'peer_wq': bf(peer_wq[l]), 'peer_k1': bf(peer_k1[l]), 'peer_k2': bf(peer_k2[l]),
            'peer_u': bf(peer_u[l]), 'peer_vt': bf(peer_v[l].T),
        }
        is_last = l == depth - 1
        mem_kv = norm_proj(mem_prompt.reshape(B * M, D_MODEL), row(norm_mem[l]),
                           bf(jnp.concatenate([w_mk[l], w_mv[l]], axis=1)))
        mk_p = mem_kv[:, :MEM_W].reshape(B, M, MEM_W)
        mv_p = mem_kv[:, MEM_W:].reshape(B, M, MEM_W)
        xp, st_p = trunk_layer(xp, S, 0, lw, None,
                               jnp.zeros((B, SHIFT_W), F32), jnp.zeros((B, RWKV_HEADS, RWKV_N, RWKV_N), F32),
                               mk_p, mv_p, final_g, is_last)
        paged = (cache_k[l], cache_v[l], cache_idx_k[l], page_table)
        xs, st_s = trunk_layer(xs, T, past, lw, paged, state_shift[l], state_wkv[l],
                               cache_mem_k[l].reshape(Bd, M, MEM_W), cache_mem_v[l].reshape(Bd, M, MEM_W), final_g, is_last)
        st_p_all.append(st_p)
        st_s_all.append(st_s)
        mem_all.append((mk_p.reshape(B, M, MEM_HEADS, MEM_HD), mv_p.reshape(B, M, MEM_HEADS, MEM_HD)))
    y_prompt, y_sample = xp, xs
    stack = lambda lst, i: jnp.stack([s[i] for s in lst])
    return (y_prompt, y_sample,
            stack(st_p_all, 0), stack(st_p_all, 1), stack(st_p_all, 2), stack(st_p_all, 3), stack(st_p_all, 4),
            stack(mem_all, 0), stack(mem_all, 1),
            stack(st_s_all, 0), stack(st_s_all, 1), stack(st_s_all, 2), stack(st_s_all, 3), stack(st_s_all, 4))
```

```python
import functools
import math

import jax
import jax.numpy as jnp
import numpy as np
from jax import lax
from jax.experimental import pallas as pl
from jax.experimental.pallas import tpu as pltpu

D_MODEL = 1024
PAGE_SIZE = 128
ATT_HEADS = 8
KV_HEADS = 4
HEAD_DIM = 64
ATT_W = ATT_HEADS * HEAD_DIM
KV_W = KV_HEADS * HEAD_DIM
IDX_HEADS = 4
IDX_DIM = 64
TOPK_MAX = 256
Q_BLOCK = 128
ROPE_THETA = 10000.0
RWKV_HEADS = 8
RWKV_N = 64
RWKV_W = RWKV_HEADS * RWKV_N
LORA_W = 64
LORA_A = 64
LORA_G = 128
LN_X_EPS = 64e-5
SHIFT_W = 3 * RWKV_W + LORA_W + LORA_A + LORA_G
MEM_HEADS = 4
MEM_HD = 128
MEM_W = MEM_HEADS * MEM_HD
PEER_HEADS = 8
PEER_DK = 128
N_KEYS = 128
PEER_TOPK = 16
PEER_BLOCK = 128
NORM_EPS = 1e-6
SPLITS = (ATT_W, KV_W, KV_W, IDX_HEADS * IDX_DIM, IDX_DIM, IDX_HEADS, SHIFT_W, 2 * D_MODEL)

LANES = 128
VMEM_LIMIT = 56 * 1024 * 1024

F32 = jnp.float32
BF16 = jnp.bfloat16


def _cparams(*sem):
    return pltpu.CompilerParams(dimension_semantics=sem, vmem_limit_bytes=VMEM_LIMIT)


def _rms(x, g):
    return x * lax.rsqrt(jnp.mean(x * x, axis=-1, keepdims=True) + NORM_EPS) * g


_SEG_Q = (0, 512)
_SEG_K = (512, 768)
_SEG_V = (768, 1024)
_SEG_QI = (1024, 1280)
_SEG_KIWI = (1280, 1408)
_SEG_FEAT = (1408, 3200)
_SEG_GATE = (3200, 5248)
_SEG_VAUG = (5248, 5760)
_PACKED_COLS = 5760


def _pack_w_in(w_in):
    pts = np.cumsum(SPLITS)[:-1].tolist()
    wq, wk, wv, wqi, wki, wwi, wfeat, wgate = jnp.split(w_in, pts, axis=-1)
    d = w_in.shape[0]
    pad = jnp.zeros((d, LANES - IDX_DIM - IDX_HEADS), w_in.dtype)
    wv_aug = jnp.concatenate([wv.reshape(d, KV_HEADS, HEAD_DIM), jnp.zeros((d, KV_HEADS, HEAD_DIM), w_in.dtype)],
                             axis=-1).reshape(d, 2 * KV_W)
    return jnp.concatenate([wq, wk, wv, wqi, wki, wwi, pad, wfeat, wgate, wv_aug], axis=-1).astype(BF16)


def _rope_lanes(y, cos, sin_signed, first_half):
    outs = []
    for c in range(y.shape[1] // LANES):
        xc = y[:, c * LANES:(c + 1) * LANES]
        partner = jnp.where(first_half, pltpu.roll(xc, 96, 1), pltpu.roll(xc, 32, 1))
        outs.append(xc * cos + partner * sin_signed)
    return outs[0] if len(outs) == 1 else jnp.concatenate(outs, axis=1)


def _in_proj_kernel(x_ref, g_ref, w_ref, inv_ref, q_ref, k_ref, v_ref, qi_ref, kiwi_ref, feat_ref, ga_ref, gb_ref,
                    *bf_refs, tm, period, offset):
    x = x_ref[...]
    h = _rms(x, g_ref[...]).astype(BF16)
    row = pl.program_id(0) * tm + lax.broadcasted_iota(jnp.int32, (tm, LANES), 0)
    pos = (row % period + offset).astype(F32)
    ang = pos * inv_ref[...]
    cos = jnp.cos(ang)
    sin = jnp.sin(ang)
    lane = lax.broadcasted_iota(jnp.int32, (tm, LANES), 1)
    first_half = (lane % HEAD_DIM) < (HEAD_DIM // 2)
    sin_signed = jnp.where(first_half, -sin, sin)

    def seg(s):
        return jnp.dot(h, w_ref[:, s[0]:s[1]], preferred_element_type=F32)

    q_ref[...] = _rope_lanes(seg(_SEG_Q), cos, sin_signed, first_half)
    k = _rope_lanes(seg(_SEG_K), cos, sin_signed, first_half)
    k_ref[...] = k
    v_ref[...] = seg(_SEG_V)
    qi_ref[...] = _rope_lanes(seg(_SEG_QI), cos, sin_signed, first_half)
    kiwi = seg(_SEG_KIWI)
    kiwi = jnp.where(lane < IDX_DIM, _rope_lanes(kiwi, cos, sin_signed, first_half), kiwi)
    kiwi_ref[...] = kiwi
    feat_ref[...] = seg(_SEG_FEAT)
    gates = jax.nn.sigmoid(seg(_SEG_GATE))
    ga_ref[...] = gates[:, :D_MODEL]
    gb_ref[...] = gates[:, D_MODEL:]
    if bf_refs:
        kbf_ref, kibf_ref, vaug_ref = bf_refs
        kbf_ref[...] = k.astype(BF16)
        kibf_ref[...] = kiwi[:, :IDX_DIM].astype(BF16)
        ones_half = jnp.where(lane >= HEAD_DIM, 1.0, 0.0)
        vaug_ref[...] = (seg(_SEG_VAUG) + jnp.concatenate([ones_half] * (2 * KV_W // LANES), axis=1)).astype(BF16)


def in_proj(x2d, norm_g, w_packed, inv_tiled, *, period, offset, attn_operands=False, tm=256):
    n = x2d.shape[0]
    assert n % tm == 0
    outs = [(w, F32) for w in (ATT_W, KV_W, KV_W, IDX_HEADS * IDX_DIM, LANES, SHIFT_W, D_MODEL, D_MODEL)]
    if attn_operands:
        outs += [(KV_W, BF16), (IDX_DIM, BF16), (2 * KV_W, BF16)]
    row = lambda w: pl.BlockSpec((tm, w), lambda i: (i, 0))
    full = lambda a: pl.BlockSpec(a.shape, lambda i: (0,) * a.ndim)
    return pl.pallas_call(
        functools.partial(_in_proj_kernel, tm=tm, period=period, offset=offset),
        grid=(n // tm,),
        in_specs=[row(D_MODEL), full(norm_g), full(w_packed), full(inv_tiled)],
        out_specs=[row(w) for w, _ in outs],
        out_shape=[jax.ShapeDtypeStruct((n, w), dt) for w, dt in outs],
        compiler_params=_cparams("parallel"),
        name="in_proj",
    )(x2d, norm_g, w_packed, inv_tiled)


def _mix_out_kernel(x_ref, oa_ref, ob_ref, ga_ref, gb_ref, pa_ref, pb_ref, wo_ref, o_ref):
    ya = jnp.dot(oa_ref[...].astype(BF16), pa_ref[...], preferred_element_type=F32)
    yb = jnp.dot(ob_ref[...].astype(BF16), pb_ref[...], preferred_element_type=F32)
    m = ga_ref[...] * ya + gb_ref[...] * yb
    o_ref[...] = x_ref[...] + jnp.dot(m.astype(BF16), wo_ref[...], preferred_element_type=F32)


def mix_out(x2d, oa, ob, ga, gb, pa, pb, wo, *, tm=256):
    n = x2d.shape[0]
    row = lambda w: pl.BlockSpec((tm, w), lambda i: (i, 0))
    full = lambda a: pl.BlockSpec(a.shape, lambda i: (0,) * a.ndim)
    return pl.pallas_call(
        _mix_out_kernel,
        grid=(n // tm,),
        in_specs=[row(D_MODEL), row(ATT_W), row(RWKV_W), row(D_MODEL), row(D_MODEL), full(pa), full(pb), full(wo)],
        out_specs=row(D_MODEL),
        out_shape=jax.ShapeDtypeStruct((n, D_MODEL), F32),
        compiler_params=_cparams("parallel"),
        name="mix_out",
    )(x2d, oa, ob, ga, gb, pa, pb, wo)


LOG2_E = math.log2(math.e)
INT_MIN = -(2 ** 31)
NEG_BIG = -1e30
IDX_SCALE = IDX_DIM ** -0.5 * IDX_HEADS ** -0.5


def _sortable(s):
    b = lax.bitcast_convert_type(s, jnp.int32)
    return b ^ ((b >> 31) & 0x7FFFFFFF)


def _lane_tile(x, reps):
    return x if reps == 1 else jnp.concatenate([x] * reps, axis=1)


def _select_threshold(keys_ref, nch, cw, rows, k_top):
    reps = cw // LANES

    def count_ge(thr_col):
        thr = jnp.broadcast_to(thr_col, (rows, LANES))

        def body(c, acc):
            kc = keys_ref[:, pl.ds(c * cw if isinstance(c, int) else pl.multiple_of(c * cw, cw), cw)]
            for u in range(reps):
                acc = acc + jnp.where(kc[:, u * LANES:(u + 1) * LANES] >= thr, 1.0, 0.0)
            return acc
        acc = jnp.zeros((rows, LANES), F32)
        if isinstance(nch, int):
            for c in range(nch):
                acc = body(c, acc)
        else:
            acc = lax.fori_loop(0, nch, body, acc)
        return jnp.sum(acc, axis=1, keepdims=True)

    def value_bit(i, t_u):
        cand = t_u | jnp.left_shift(jnp.int32(1), 31 - i)
        return jnp.where(count_ge(cand ^ INT_MIN) >= k_top, cand, t_u)

    t_u = lax.fori_loop(0, 32, value_bit, jnp.zeros((rows, 1), jnp.int32))
    tau = t_u ^ INT_MIN
    return tau, k_top - count_ge(tau + 1)


def _prefix_ones(n):
    return (jnp.arange(n)[:, None] <= jnp.arange(n)[None, :]).astype(BF16)


def _selection_bias(kc, tau_b, need_b, seen, tri):
    reps = kc.shape[1] // LANES
    eq = kc == _lane_tile(tau_b, reps)
    eq01 = jnp.where(eq, 1.0, 0.0)
    rank = jnp.dot(eq01.astype(BF16), tri, preferred_element_type=F32) + _lane_tile(seen, reps)
    tie_ok = jnp.where(eq, jnp.where(rank <= _lane_tile(need_b, reps), 0.0, NEG_BIG), NEG_BIG)
    bias = jnp.where(kc > _lane_tile(tau_b, reps), 0.0, tie_ok)
    return bias, seen + jnp.sum(eq01, axis=1, keepdims=True)


def _dsa_prompt_kernel(q_ref, qi_ref, kiwi_ref, k_ref, v_ref, ki_ref, tri_ref, o_ref,
                       keys_ref, wb_ref, m_ref, acc_ref, *, k_top, cw):
    tq = Q_BLOCK
    j = pl.program_id(1)
    nch = (j * tq + tq + cw - 1) // cw
    reps = cw // LANES
    t_b = j * tq + lax.broadcasted_iota(jnp.int32, (tq, LANES), 0)
    lane = lax.broadcasted_iota(jnp.int32, (tq, LANES), 1)

    qi = qi_ref[0]
    qi4 = jnp.concatenate([qi[:, h * IDX_DIM:(h + 1) * IDX_DIM] for h in range(IDX_HEADS)], axis=0).astype(BF16)
    wi = kiwi_ref[0][:, IDX_DIM:IDX_DIM + IDX_HEADS]
    for h in range(IDX_HEADS):
        wb_ref[h] = jnp.broadcast_to(wi[:, h:h + 1], (tq, LANES))

    def score_body(c, _):
        off = pl.multiple_of(c * cw, cw)
        kic = ki_ref[0, pl.ds(off, cw), :]
        dots = lax.dot_general(qi4, kic, (((1,), (1,)), ((), ())), preferred_element_type=F32)
        parts = []
        for u in range(reps):
            su = jnp.zeros((tq, LANES), F32)
            for h in range(IDX_HEADS):
                su = su + wb_ref[h] * jnp.maximum(dots[h * tq:(h + 1) * tq, u * LANES:(u + 1) * LANES], 0.0)
            kpos = lane + (off + u * LANES)
            parts.append(_sortable(jnp.where(kpos <= t_b, su * IDX_SCALE, -jnp.inf)))
        keys_ref[:, pl.ds(off, cw)] = _lane_tile(parts[0], 1) if reps == 1 else jnp.concatenate(parts, axis=1)
        return 0

    lax.fori_loop(0, nch, score_body, 0)

    tau, need = _select_threshold(keys_ref, nch, cw, tq, k_top)
    tau_b = jnp.broadcast_to(tau, (tq, LANES))
    need_b = jnp.broadcast_to(need, (tq, LANES))
    pos_b = _lane_tile(t_b - lane, reps) - jnp.concatenate(
        [jnp.full((tq, LANES), u * LANES, jnp.int32) for u in range(reps)], axis=1)

    q = q_ref[0] * (HEAD_DIM ** -0.5 * LOG2_E)
    qs = []
    for n in range(KV_HEADS):
        qs.append(jnp.concatenate([q[:, (2 * n) * HEAD_DIM:(2 * n + 1) * HEAD_DIM],
                                   q[:, (2 * n + 1) * HEAD_DIM:(2 * n + 2) * HEAD_DIM]], axis=0).astype(BF16))
    m_ref[...] = jnp.full(m_ref.shape, NEG_BIG, F32)
    acc_ref[...] = jnp.zeros(acc_ref.shape, F32)

    def attn_body(c, seen):
        off = pl.multiple_of(c * cw, cw)
        bias, seen = _selection_bias(keys_ref[:, pl.ds(off, cw)], tau_b, need_b, seen, tri_ref[...])
        bias = jnp.where(off <= pos_b, bias, NEG_BIG)
        bias2 = jnp.concatenate([bias, bias], axis=0)
        ss = []
        for n in range(KV_HEADS):
            kn = k_ref[0, pl.ds(off, cw), n * HEAD_DIM:(n + 1) * HEAD_DIM]
            ss.append(lax.dot_general(qs[n], kn, (((1,), (1,)), ((), ())), preferred_element_type=F32) + bias2)
        ps, alphas = [], []
        for n in range(KV_HEADS):
            m_prev = m_ref[n]
            m_new = jnp.maximum(m_prev, jnp.max(ss[n], axis=1, keepdims=True))
            ps.append(jnp.exp2(ss[n] - _lane_tile(m_new, reps)).astype(BF16))
            alphas.append(jnp.exp2(m_prev - m_new))
            m_ref[n] = m_new
        for n in range(KV_HEADS):
            vn = v_ref[0, pl.ds(off, cw), n * LANES:(n + 1) * LANES]
            acc_ref[n] = alphas[n] * acc_ref[n] + jnp.dot(ps[n], vn, preferred_element_type=F32)
        return seen

    lax.fori_loop(0, nch, attn_body, jnp.zeros((tq, LANES), F32))

    outs = []
    for n in range(KV_HEADS):
        acc = acc_ref[n]
        o = acc[:, :HEAD_DIM] / acc[:, HEAD_DIM:]
        outs += [o[:tq], o[tq:]]
    o_ref[0] = jnp.concatenate(outs, axis=1)


def dsa_prompt_attn(q, qi, kiwi, k_bf, v_aug, ki_bf, *, cw=512):
    B, S, _ = q.shape
    cw = min(cw, S)
    assert S % cw == 0 and cw % LANES == 0 and S % Q_BLOCK == 0
    k_top = min(TOPK_MAX, S // 4)
    blk = lambda w: pl.BlockSpec((1, Q_BLOCK, w), lambda b, j: (b, j, 0))
    res = lambda w: pl.BlockSpec((1, S, w), lambda b, j: (b, 0, 0), pipeline_mode=pl.Buffered(1))
    return pl.pallas_call(
        functools.partial(_dsa_prompt_kernel, k_top=k_top, cw=cw),
        grid=(B, S // Q_BLOCK),
        in_specs=[blk(ATT_W), blk(IDX_HEADS * IDX_DIM), blk(LANES), res(KV_W), res(2 * KV_W), res(IDX_DIM),
                  pl.BlockSpec((cw, cw), lambda b, j: (0, 0), pipeline_mode=pl.Buffered(1))],
        out_specs=blk(ATT_W),
        out_shape=jax.ShapeDtypeStruct((B, S, ATT_W), F32),
        scratch_shapes=[pltpu.VMEM((Q_BLOCK, S), jnp.int32),
                        pltpu.VMEM((IDX_HEADS, Q_BLOCK, LANES), F32),
                        pltpu.VMEM((KV_HEADS, 2 * Q_BLOCK, LANES), F32),
                        pltpu.VMEM((KV_HEADS, 2 * Q_BLOCK, LANES), F32)],
        compiler_params=_cparams("parallel", "arbitrary"),
        name="dsa_prompt",
    )(q, qi, kiwi, k_bf, v_aug, ki_bf, _prefix_ones(cw))


SAMPLE_UNROLL = 4


def _dsa_sample_kernel(pt_ref, q_ref, qi_ref, kiwi_ref, kn_ref, vn_ref, tri_ref, ck_hbm, cv_hbm, cik_hbm, o_ref,
                       kbuf2, vbuf2, ibuf2, sems, keys_ref, s_ref, *, n_pages, k_top, cw):
    b = pl.program_id(0)
    n_b = pl.num_programs(0)
    t_new = q_ref.shape[1]
    past = n_pages * PAGE_SIZE
    ppc = cw // PAGE_SIZE
    nch = n_pages // ppc
    reps = cw // LANES
    slot = b % 2

    streams = ((cik_hbm, ibuf2), (ck_hbm, kbuf2), (cv_hbm, vbuf2))

    def page_copy(which, seq, sl, p):
        src, dst = streams[which]
        return pltpu.make_async_copy(src.at[pt_ref[seq, p]], dst.at[sl, p], sems.at[sl, which])

    def start_pages(seq, sl):
        def body(p, _):
            for which in range(len(streams)):
                page_copy(which, seq, sl, p).start()
            return 0
        lax.fori_loop(0, n_pages, body, 0)

    def wait_pages(which):
        def body(p, _):
            page_copy(which, b, slot, p).wait()
            return 0
        lax.fori_loop(0, n_pages, body, 0)

    @pl.when(b == 0)
    def _():
        start_pages(b, slot)

    @pl.when(b + 1 < n_b)
    def _():
        start_pages(b + 1, 1 - slot)

    ibuf = ibuf2.at[slot]
    kbuf = kbuf2.at[slot]
    vbuf = vbuf2.at[slot]

    lane = lax.broadcasted_iota(jnp.int32, (t_new, LANES), 1)
    t_b = lax.broadcasted_iota(jnp.int32, (t_new, LANES), 0)
    qi = qi_ref[0]
    qi4 = jnp.concatenate([qi[:, h * IDX_DIM:(h + 1) * IDX_DIM] for h in range(IDX_HEADS)], axis=0).astype(BF16)
    kiwi = kiwi_ref[0]
    wi = kiwi[:, IDX_DIM:IDX_DIM + IDX_HEADS]
    wb = [jnp.broadcast_to(wi[:, h:h + 1], (t_new, LANES)) for h in range(IDX_HEADS)]

    def combine(dots_u):
        su = jnp.zeros((t_new, LANES), F32)
        for h in range(IDX_HEADS):
            su = su + wb[h] * jnp.maximum(dots_u[h * t_new:(h + 1) * t_new], 0.0)
        return su * IDX_SCALE

    wait_pages(0)

    def score_body(c, _):
        kic = ibuf[pl.ds(c * ppc, ppc)].reshape(cw, IDX_DIM).astype(BF16)
        dots = lax.dot_general(qi4, kic, _NT, preferred_element_type=F32)
        parts = [_sortable(combine(dots[:, u * LANES:(u + 1) * LANES])) for u in range(reps)]
        keys_ref[:, pl.ds(pl.multiple_of(c * cw, cw), cw)] = jnp.concatenate(parts, axis=1)
        return 0

    lax.fori_loop(0, nch, score_body, 0, unroll=SAMPLE_UNROLL)
    ki_new = jnp.concatenate([kiwi[:, :IDX_DIM], jnp.zeros((LANES - t_new, IDX_DIM), F32)], axis=0).astype(BF16)
    dots_new = lax.dot_general(qi4, ki_new, _NT, preferred_element_type=F32)
    s_new = jnp.where(lane <= t_b, combine(dots_new), -jnp.inf)
    neg = jnp.full((t_new, LANES), -jnp.inf, F32)
    keys_ref[:, pl.ds(past, cw)] = _sortable(jnp.concatenate([s_new] + [neg] * (reps - 1), axis=1))

    tau, need = _select_threshold(keys_ref, nch + 1, cw, t_new, k_top)
    tau_b = jnp.broadcast_to(tau, (t_new, LANES))
    need_b = jnp.broadcast_to(need, (t_new, LANES))

    q = q_ref[0] * (HEAD_DIM ** -0.5 * LOG2_E)
    qs = []
    for n in range(KV_HEADS):
        qs.append(jnp.concatenate([q[:, (2 * n) * HEAD_DIM:(2 * n + 1) * HEAD_DIM],
                                   q[:, (2 * n + 1) * HEAD_DIM:(2 * n + 2) * HEAD_DIM]], axis=0).astype(BF16))
    rows2 = 2 * t_new

    def logits(kc_bf, bias, off, mx):
        width = bias.shape[1]
        bias2 = jnp.concatenate([bias, bias], axis=0)
        new_mx = []
        for n in range(KV_HEADS):
            s = lax.dot_general(qs[n], kc_bf[:, n * HEAD_DIM:(n + 1) * HEAD_DIM], _NT, preferred_element_type=F32) + bias2
            s_ref[n, :, pl.ds(off, width)] = s
            m = mx[n]
            for u in range(width // LANES):
                m = jnp.maximum(m, s[:, u * LANES:(u + 1) * LANES])
            new_mx.append(m)
        return tuple(new_mx)

    wait_pages(1)

    def logit_body(c, carry):
        mx, seen = carry
        off = pl.multiple_of(c * cw, cw)
        kc = kbuf[pl.ds(c * ppc, ppc)].reshape(cw, KV_W).astype(BF16)
        bias, seen = _selection_bias(keys_ref[:, pl.ds(off, cw)], tau_b, need_b, seen, tri_ref[...])
        return logits(kc, bias, off, mx), seen

    mx, seen = lax.fori_loop(0, nch, logit_body,
                             (tuple(jnp.full((rows2, LANES), NEG_BIG, F32) for _ in range(KV_HEADS)),
                              jnp.zeros((t_new, LANES), F32)), unroll=SAMPLE_UNROLL)
    pad_rows = jnp.zeros((LANES - t_new, KV_W), F32)
    kn_pad = jnp.concatenate([kn_ref[0], pad_rows], axis=0).astype(BF16)
    vn_pad = jnp.concatenate([vn_ref[0], pad_rows], axis=0).astype(BF16)
    bias_new, _ = _selection_bias(keys_ref[:, pl.ds(past, LANES)], tau_b, need_b, seen, tri_ref[:LANES, :LANES])
    mx = logits(kn_pad, jnp.where(lane <= t_b, bias_new, NEG_BIG), past, mx)
    m_row = [jnp.broadcast_to(jnp.max(m, axis=1, keepdims=True), (rows2, LANES)) for m in mx]

    def weighted(vc_bf, off, width, carry):
        accs, sums = carry
        new_accs, new_sums = [], []
        for n in range(KV_HEADS):
            p = jnp.exp2(s_ref[n, :, pl.ds(off, width)] - _lane_tile(m_row[n], width // LANES))
            l = sums[n]
            for u in range(width // LANES):
                l = l + p[:, u * LANES:(u + 1) * LANES]
            new_sums.append(l)
            new_accs.append(accs[n] + jnp.dot(p.astype(BF16), vc_bf[:, n * HEAD_DIM:(n + 1) * HEAD_DIM],
                                              preferred_element_type=F32))
        return tuple(new_accs), tuple(new_sums)

    wait_pages(2)

    def pv_body(c, carry):
        vc = vbuf[pl.ds(c * ppc, ppc)].reshape(cw, KV_W).astype(BF16)
        return weighted(vc, pl.multiple_of(c * cw, cw), cw, carry)

    zero = lambda w: tuple(jnp.zeros((rows2, w), F32) for _ in range(KV_HEADS))
    carry = lax.fori_loop(0, nch, pv_body, (zero(HEAD_DIM), zero(LANES)), unroll=SAMPLE_UNROLL)
    accs, sums = weighted(vn_pad, past, LANES, carry)

    outs = []
    for n in range(KV_HEADS):
        o = accs[n] / jnp.sum(sums[n], axis=1, keepdims=True)
        outs += [o[:t_new], o[t_new:]]
    o_ref[0] = jnp.concatenate(outs, axis=1)


def dsa_sample_attn(q, qi, kiwi, k_new, v_new, cache_k, cache_v, cache_idx_k, page_table, *, cw=512):
    Bd, T, _ = q.shape
    n_pages = page_table.shape[1]
    past = n_pages * PAGE_SIZE
    cw = min(cw, past)
    assert cw % PAGE_SIZE == 0 and past % cw == 0 and T <= LANES
    k_top = min(TOPK_MAX, (past + T) // 4)
    n_phys = cache_k.shape[0]
    blk = lambda w: pl.BlockSpec((1, T, w), lambda b, pt: (b, 0, 0))
    hbm = pl.BlockSpec(memory_space=pl.ANY)
    gs = pltpu.PrefetchScalarGridSpec(
        num_scalar_prefetch=1,
        grid=(Bd,),
        in_specs=[blk(ATT_W), blk(IDX_HEADS * IDX_DIM), blk(LANES), blk(KV_W), blk(KV_W),
                  pl.BlockSpec((cw, cw), lambda b, pt: (0, 0)), hbm, hbm, hbm],
        out_specs=blk(ATT_W),
        scratch_shapes=[pltpu.VMEM((2, n_pages, PAGE_SIZE, KV_W), F32),
                        pltpu.VMEM((2, n_pages, PAGE_SIZE, KV_W), F32),
                        pltpu.VMEM((2, n_pages, PAGE_SIZE, IDX_DIM), F32),
                        pltpu.SemaphoreType.DMA((2, 3)),
                        pltpu.VMEM((T, past + cw), jnp.int32),
                        pltpu.VMEM((KV_HEADS, 2 * T, past + LANES), F32)])
    return pl.pallas_call(
        functools.partial(_dsa_sample_kernel, n_pages=n_pages, k_top=k_top, cw=cw),
        grid_spec=gs,
        out_shape=jax.ShapeDtypeStruct((Bd, T, ATT_W), F32),
        compiler_params=_cparams("arbitrary"),
        name="dsa_sample",
    )(page_table, q, qi, kiwi, k_new, v_new, _prefix_ones(cw),
      cache_k.reshape(n_phys, PAGE_SIZE, KV_W), cache_v.reshape(n_phys, PAGE_SIZE, KV_W), cache_idx_k)


def _cross_attn_kernel(x_ref, g_ref, wq_ref, mk_ref, mv_ref, wo_ref, o_ref):
    x = x_ref[0]
    q = jnp.dot(_rms(x, g_ref[...]).astype(BF16), wq_ref[...], preferred_element_type=F32) * (MEM_HD ** -0.5)
    q = q.astype(BF16)
    mk = mk_ref[0].astype(BF16)
    mv = mv_ref[0].astype(BF16)
    outs = []
    for h in range(MEM_HEADS):
        sl = slice(h * MEM_HD, (h + 1) * MEM_HD)
        s = lax.dot_general(q[:, sl], mk[:, sl], _NT, preferred_element_type=F32)
        p = jnp.exp(s - jnp.max(s, axis=1, keepdims=True))
        p = p / jnp.sum(p, axis=1, keepdims=True)
        outs.append(jnp.dot(p.astype(BF16), mv[:, sl], preferred_element_type=F32))
    o = jnp.concatenate(outs, axis=1).astype(BF16)
    o_ref[0] = x + jnp.dot(o, wo_ref[...], preferred_element_type=F32)


def cross_attn_mem(x, norm_g, wq_bf, mk, mv, wo_bf, *, tm=512):
    B, T, _ = x.shape
    tm = min(tm, T)
    assert T % tm == 0
    M = mk.shape[1]
    full = lambda a: pl.BlockSpec(a.shape, lambda b, i: (0,) * a.ndim)
    mem = pl.BlockSpec((1, M, MEM_W), lambda b, i: (b, 0, 0))
    return pl.pallas_call(
        _cross_attn_kernel,
        grid=(B, T // tm),
        in_specs=[pl.BlockSpec((1, tm, D_MODEL), lambda b, i: (b, i, 0)), full(norm_g), full(wq_bf), mem, mem, full(wo_bf)],
        out_specs=pl.BlockSpec((1, tm, D_MODEL), lambda b, i: (b, i, 0)),
        out_shape=jax.ShapeDtypeStruct(x.shape, F32),
        compiler_params=_cparams("parallel", "arbitrary"),
        name="cross_attn",
    )(x, norm_g, wq_bf, mk, mv, wo_bf)


def _norm_proj_kernel(x_ref, g_ref, w_ref, o_ref):
    o_ref[...] = jnp.dot(_rms(x_ref[...], g_ref[...]).astype(BF16), w_ref[...], preferred_element_type=F32)


def norm_proj(x2d, norm_g, w_bf, *, tm=256):
    n = x2d.shape[0]
    tm = min(tm, n)
    assert n % tm == 0
    return pl.pallas_call(
        _norm_proj_kernel,
        grid=(n // tm,),
        in_specs=[pl.BlockSpec((tm, x2d.shape[1]), lambda i: (i, 0)), pl.BlockSpec(norm_g.shape, lambda i: (0, 0)),
                  pl.BlockSpec(w_bf.shape, lambda i: (0, 0))],
        out_specs=pl.BlockSpec((tm, w_bf.shape[1]), lambda i: (i, 0)),
        out_shape=jax.ShapeDtypeStruct((n, w_bf.shape[1]), F32),
        compiler_params=_cparams("parallel"),
        name="norm_proj",
    )(x2d, norm_g, w_bf)


PEER_HALF = PEER_DK // 2
PEER_SUB = 256
PEER_TE = 2 * PEER_SUB
PEER_CAND_CAP = tuple(PEER_TOPK // (a + 1) for a in range(PEER_TOPK))
PEER_CAND_ROWS = -(-sum(PEER_CAND_CAP) // 8) * 8


def _extract_top(work_ref, n_keys, tn, on_pick):
    sub = lax.broadcasted_iota(jnp.int32, (n_keys, tn), 0).astype(F32)
    for a in range(PEER_TOPK):
        w = work_ref[...]
        m = jnp.max(w, axis=0, keepdims=True)
        idx = jnp.min(jnp.where(w == m, sub, float(n_keys)), axis=0, keepdims=True)
        hit = sub == idx
        work_ref[...] = jnp.where(hit, -jnp.inf, w)
        on_pick(a, m, hit)


def _peer_kernel(x_ref, g_ref, gf_ref, wq_ref, k1_ref, k2_ref, u_ref, vt_ref, o_ref,
                 xn_ref, s_ref, rank_ref, vals_ref, work_ref, cand_ref, c_ref, l_ref, e2_ref, r2_ref, acc_ref, ht_ref, w_ref,
                 *, tn, final_norm):
    e = pl.program_id(1)
    n_e = pl.num_programs(1)

    @pl.when(e == 0)
    def _route():
        xn = _rms(x_ref[...], g_ref[...]).astype(BF16)
        xn_ref[...] = xn
        q = jnp.dot(xn, wq_ref[...], preferred_element_type=F32).astype(BF16)
        for h in range(PEER_HEADS):
            q1 = q[:, h * PEER_DK:h * PEER_DK + PEER_HALF]
            q2 = q[:, h * PEER_DK + PEER_HALF:(h + 1) * PEER_DK]
            nt = (((1,), (1,)), ((), ()))
            s_ref[h] = lax.dot_general(k1_ref[h], q1, nt, preferred_element_type=F32)
            s_ref[PEER_HEADS + h] = lax.dot_general(k2_ref[h], q2, nt, preferred_element_type=F32)

        def top_keys(hh, _):
            work_ref[...] = s_ref[hh]
            rank_ref[hh] = jnp.full((N_KEYS, tn), float(PEER_TOPK), F32)

            def pick(a, m, hit):
                vals_ref[hh, a:a + 1, :] = m
                rank_ref[hh] = jnp.where(hit, float(a), rank_ref[hh])
            _extract_top(work_ref, N_KEYS, tn, pick)
            return 0

        lax.fori_loop(0, 2 * PEER_HEADS, top_keys, 0)

        def gates(h, _):
            v1 = vals_ref[h]
            v2 = vals_ref[PEER_HEADS + h]
            rows = [v1[a:a + 1, :] + v2[0:cap, :] for a, cap in enumerate(PEER_CAND_CAP)]
            rows.append(jnp.full((PEER_CAND_ROWS - sum(PEER_CAND_CAP), tn), -jnp.inf, F32))
            cand_ref[...] = jnp.concatenate(rows, axis=0)
            _extract_top(cand_ref, PEER_CAND_ROWS, tn, lambda a, m, hit: None)
            picked = jnp.where(cand_ref[...] == -jnp.inf, 1.0, 0.0)
            e1 = jnp.exp(v1 - v1[0:1, :])
            e2 = jnp.exp(v2 - v2[0:1, :])
            z = jnp.zeros((1, tn), F32)
            r1 = rank_ref[h]
            lfull = jnp.zeros((N_KEYS, tn), F32)
            off = 0
            for a, cap in enumerate(PEER_CAND_CAP):
                sel = picked[off:off + cap, :]
                off += cap
                z = z + e1[a:a + 1, :] * jnp.sum(sel * e2[0:cap, :], axis=0, keepdims=True)
                lfull = lfull + jnp.where(r1 == float(a), jnp.sum(sel, axis=0, keepdims=True), 0.0)
            l_ref[h] = lfull
            c_ref[h] = jnp.where(r1 < float(PEER_TOPK), jnp.exp(s_ref[h] - v1[0:1, :]) / z, 0.0)
            e2_ref[h] = jnp.exp(s_ref[PEER_HEADS + h] - v2[0:1, :]).astype(BF16)
            r2_ref[h] = rank_ref[PEER_HEADS + h].astype(BF16)
            return 0

        lax.fori_loop(0, PEER_HEADS, gates, 0)
        acc_ref[...] = jnp.zeros(acc_ref.shape, F32)

        ht_ref[...] = jnp.zeros(ht_ref.shape, F32)
        w_ref[...] = jnp.zeros(w_ref.shape, BF16)

    n_sub = 2 * (n_e - 1)
    keys_per_sub = PEER_SUB // N_KEYS

    def gated(ht, sub):
        live = jnp.where(jnp.logical_and(sub >= 0, sub < n_sub), 1.0, 0.0)
        base = jnp.clip(sub, 0, n_sub - 1) * keys_per_sub
        parts = []
        zero = jnp.zeros((), BF16)
        for ii in range(keys_per_sub):
            g = jnp.zeros((N_KEYS, tn), BF16)
            for h in range(PEER_HEADS):
                cb = jnp.broadcast_to((c_ref[h, pl.ds(base + ii, 1), :] * live).astype(BF16), (N_KEYS, tn))
                lb = jnp.broadcast_to(l_ref[h, pl.ds(base + ii, 1), :].astype(BF16), (N_KEYS, tn))
                g = g + cb * jnp.where(r2_ref[h] < lb, e2_ref[h], zero)
            hi = ht[ii * N_KEYS:(ii + 1) * N_KEYS]
            act = 0.5 * hi * (1.0 + lax.erf(hi * (2.0 ** -0.5)))
            parts.append(g * act.astype(BF16))
        return parts[0] if len(parts) == 1 else jnp.concatenate(parts, axis=0)

    nt = (((1,), (1,)), ((), ()))
    acc_ref[...] += lax.dot_general(vt_ref[:PEER_SUB, :], w_ref[...], _TN, preferred_element_type=F32)
    ht_new = lax.dot_general(u_ref[:PEER_SUB, :], xn_ref[...], nt, preferred_element_type=F32)
    w_mid = gated(ht_ref[...], 2 * e - 1)
    ht_ref[...] = lax.dot_general(u_ref[PEER_SUB:, :], xn_ref[...], nt, preferred_element_type=F32)
    w_ref[...] = gated(ht_new, 2 * e)
    acc_ref[...] += lax.dot_general(vt_ref[PEER_SUB:, :], w_mid, _TN, preferred_element_type=F32)

    @pl.when(e == n_e - 1)
    def _finish():
        y = x_ref[...] + acc_ref[...].T
        if final_norm:
            y = _rms(y, gf_ref[...])
        o_ref[...] = y


def peer_ffn(x2d, norm_g, final_g, wq_bf, k1_bf, k2_bf, u_bf, vt_bf, *, final_norm, tn=512):
    n = x2d.shape[0]
    tn = min(tn, n)
    assert n % tn == 0 and tn % LANES == 0
    n_blk = u_bf.shape[0] // PEER_TE
    full = lambda a: pl.BlockSpec(a.shape, lambda t, e: (0,) * a.ndim)
    return pl.pallas_call(
        functools.partial(_peer_kernel, tn=tn, final_norm=final_norm),
        grid=(n // tn, n_blk + 1),
        in_specs=[pl.BlockSpec((tn, D_MODEL), lambda t, e: (t, 0)), full(norm_g), full(final_g), full(wq_bf),
                  full(k1_bf), full(k2_bf),
                  pl.BlockSpec((PEER_TE, D_MODEL), lambda t, e: (jnp.minimum(e, n_blk - 1), 0)),
                  pl.BlockSpec((PEER_TE, D_MODEL), lambda t, e: (jnp.maximum(e - 1, 0), 0))],
        out_specs=pl.BlockSpec((tn, D_MODEL), lambda t, e: (t, 0)),
        out_shape=jax.ShapeDtypeStruct((n, D_MODEL), F32),
        scratch_shapes=[pltpu.VMEM((tn, D_MODEL), BF16),
                        pltpu.VMEM((2 * PEER_HEADS, N_KEYS, tn), F32),
                        pltpu.VMEM((2 * PEER_HEADS, N_KEYS, tn), F32),
                        pltpu.VMEM((2 * PEER_HEADS, PEER_TOPK, tn), F32),
                        pltpu.VMEM((N_KEYS, tn), F32),
                        pltpu.VMEM((PEER_CAND_ROWS, tn), F32),
                        pltpu.VMEM((PEER_HEADS, N_KEYS, tn), F32),
                        pltpu.VMEM((PEER_HEADS, N_KEYS, tn), F32),
                        pltpu.VMEM((PEER_HEADS, N_KEYS, tn), BF16),
                        pltpu.VMEM((PEER_HEADS, N_KEYS, tn), BF16),
                        pltpu.VMEM((D_MODEL, tn), F32),
                        pltpu.VMEM((PEER_SUB, tn), F32),
                        pltpu.VMEM((PEER_SUB, tn), BF16)],
        compiler_params=_cparams("parallel", "arbitrary"),
        name="peer_ffn",
    )(x2d, norm_g, final_g, wq_bf, k1_bf, k2_bf, u_bf, vt_bf)


RWKV_GROUP = 4
RWKV_GW = RWKV_GROUP * RWKV_N
RWKV_CHUNK = 64


def _split_bf16(x, terms):
    parts = []
    for _ in range(terms):
        p = x.astype(BF16)
        parts.append(p)
        x = x - p.astype(F32)
    return parts


def _dot3(a, b, dims=(((1,), (0,)), ((), ()))):
    ah, al = _split_bf16(a, 2)
    bh, bl = _split_bf16(b, 2)
    d = lambda x, y: lax.dot_general(x, y, dims, preferred_element_type=F32)
    return d(ah, bh) + d(al, bh) + d(ah, bl)


def _dot1(a, b, dims=(((1,), (0,)), ((), ()))):
    return lax.dot_general(a.astype(BF16), b.astype(BF16), dims, preferred_element_type=F32)


def _dot_exact_rhs(a, b_exact_bf16):
    return sum(jnp.dot(p, b_exact_bf16, preferred_element_type=F32) for p in _split_bf16(a, 3))


def _dot_exact_lhs(a_exact_bf16, b):
    return sum(jnp.dot(a_exact_bf16, p, preferred_element_type=F32) for p in _split_bf16(b, 3))


_NT = (((1,), (1,)), ((), ()))
_TN = (((0,), (0,)), ((), ()))


def _rwkv_kernel(feat_ref, prev_ref, s0t_ref, mu_ref, w0_ref, w2_ref, a0_ref, a2_ref, g2_ref, kk_ref, ka_ref, rk_ref,
                 lng_ref, lnb_ref, hsum_ref, tri_ref, o_ref, sfin_ref, state_ref, carry_ref, *, c_len, n_double):
    c = pl.program_id(1)
    n_c = pl.num_programs(1)
    n4 = RWKV_GROUP * c_len

    @pl.when(c == 0)
    def _init():
        carry_ref[0:1, :] = prev_ref[0]
        state_ref[...] = jnp.zeros(state_ref.shape, F32)
        for h in range(RWKV_HEADS):
            g, hh = divmod(h, RWKV_GROUP)
            state_ref[g, hh * RWKV_N:(hh + 1) * RWKV_N, hh * RWKV_N:(hh + 1) * RWKV_N] = s0t_ref[0, h]

    feat = feat_ref[0]
    row = lax.broadcasted_iota(jnp.int32, feat.shape, 0)
    shifted = jnp.where(row == 0, jnp.broadcast_to(carry_ref[0:1, :], feat.shape), pltpu.roll(feat, 1, 0))
    carry_ref[0:1, :] = feat[c_len - 1:c_len, :]
    mixed = feat + (shifted - feat) * mu_ref[...]
    r = mixed[:, 0:RWKV_W]
    k = mixed[:, RWKV_W:2 * RWKV_W]
    v = mixed[:, 2 * RWKV_W:3 * RWKV_W]
    o1 = 3 * RWKV_W
    wd = mixed[:, o1:o1 + LORA_W]
    ad = mixed[:, o1 + LORA_W:o1 + LORA_W + LORA_A]
    gd = mixed[:, o1 + LORA_W + LORA_A:]

    zw = -(w0_ref[...] + _dot3(jnp.tanh(wd), w2_ref[...]))
    log_w = -(jnp.maximum(zw, 0.0) + jnp.log1p(jnp.exp(-jnp.abs(zw)))) - 0.5
    dlog = -jnp.exp(log_w)
    a = jax.nn.sigmoid(a0_ref[...] + _dot3(ad, a2_ref[...]))
    gate = _dot3(jax.nn.sigmoid(gd), g2_ref[...])
    hsum = hsum_ref[...]
    kk = k * kk_ref[...]
    kk = kk / jnp.maximum(jnp.sqrt(_dot_exact_rhs(kk * kk, hsum)), 1e-12)
    k_eff = k * (1.0 + (a - 1.0) * ka_ref[...])

    cum = _dot_exact_lhs(tri_ref[...], dlog)
    g_in = jnp.exp(cum)
    g_inv = jnp.exp(-cum)
    kk_t = kk * jnp.exp(cum - dlog)
    b_h = kk * a * g_inv
    k_h = k_eff * g_inv
    r_t = r * g_in
    g_last = g_in[c_len - 1:c_len, :]

    rr = lax.broadcasted_iota(jnp.int32, (n4, n4), 0)
    cc = lax.broadcasted_iota(jnp.int32, (n4, n4), 1)
    eye = jnp.where(rr == cc, 1.0, 0.0)
    lane_g = lax.broadcasted_iota(jnp.int32, (c_len, RWKV_GW), 1) // RWKV_N
    rr_g = lax.broadcasted_iota(jnp.int32, (RWKV_GW, RWKV_GW), 0)
    cc_g = lax.broadcasted_iota(jnp.int32, (RWKV_GW, RWKV_GW), 1)

    groups = range(RWKV_HEADS // RWKV_GROUP)
    each = lambda f: [f(g) for g in groups]

    def stack(x, g):
        xg = x[:, g * RWKV_GW:(g + 1) * RWKV_GW]
        return jnp.concatenate([jnp.where(lane_g == h, xg, 0.0) for h in range(RWKV_GROUP)], axis=0)

    kks, rs, bs = each(lambda g: stack(kk_t, g)), each(lambda g: stack(r_t, g)), each(lambda g: stack(b_h, g))
    ks, vs = each(lambda g: stack(k_h, g)), each(lambda g: stack(v, g))
    l_b = each(lambda g: jnp.where(rr > cc, _dot3(kks[g], bs[g], _NT), 0.0))
    l_k = each(lambda g: jnp.where(rr > cc, _dot3(kks[g], ks[g], _NT), 0.0))
    m_b = each(lambda g: jnp.where(rr >= cc, _dot1(rs[g], bs[g], _NT), 0.0))
    m_k = each(lambda g: jnp.where(rr >= cc, _dot1(rs[g], ks[g], _NT), 0.0))
    npow = each(lambda g: -l_b[g])
    t_inv = each(lambda g: eye + npow[g])
    for _ in range(n_double):
        npow = each(lambda g: _dot3(npow[g], npow[g]))
        t_inv = each(lambda g: t_inv[g] + _dot3(t_inv[g], npow[g]))
    w1 = each(lambda g: -_dot3(t_inv[g], kks[g]))
    lkv = each(lambda g: _dot3(l_k[g], vs[g]))
    u1 = each(lambda g: -_dot3(t_inv[g], lkv[g]))
    gl = each(lambda g: g_last[:, g * RWKV_GW:(g + 1) * RWKV_GW])
    bsg = each(lambda g: bs[g] * gl[g])
    ksg = each(lambda g: ks[g] * gl[g])
    g_mat = each(lambda g: jnp.where(rr_g == cc_g, jnp.broadcast_to(gl[g], (RWKV_GW, RWKV_GW)), 0.0)
                 + _dot3(bsg[g], w1[g], _TN))
    h_mat = each(lambda g: _dot3(bsg[g], u1[g], _TN) + _dot3(ksg[g], vs[g], _TN))
    r_y = each(lambda g: rs[g] + _dot1(m_b[g], w1[g]))
    y_0 = each(lambda g: _dot1(m_b[g], u1[g]) + _dot1(m_k[g], vs[g]))
    s_t = each(lambda g: state_ref[g])
    y_st = each(lambda g: _dot1(r_y[g], s_t[g]) + y_0[g])
    s_new = each(lambda g: _dot3(g_mat[g], s_t[g]) + h_mat[g])
    for g in groups:
        state_ref[g] = s_new[g]
    ys = each(lambda g: sum(y_st[g][h * c_len:(h + 1) * c_len] for h in range(RWKV_GROUP)))
    y = jnp.concatenate(ys, axis=1)

    inv_n = 1.0 / RWKV_N
    mean = _dot_exact_rhs(y, hsum) * inv_n
    yc = y - mean
    var = _dot_exact_rhs(yc * yc, hsum) * inv_n
    yn = yc * lax.rsqrt(var + LN_X_EPS) * lng_ref[...] + lnb_ref[...]
    bonus = _dot_exact_rhs(r * k_eff * rk_ref[...], hsum) * v
    o_ref[0] = (yn + bonus) * gate

    @pl.when(c == n_c - 1)
    def _fin():
        for h in range(RWKV_HEADS):
            g, hh = divmod(h, RWKV_GROUP)
            sfin_ref[0, h] = state_ref[g, hh * RWKV_N:(hh + 1) * RWKV_N, hh * RWKV_N:(hh + 1) * RWKV_N]


def rwkv_mix(feat, prev, s0, lw, *, c_len):
    B, T, _ = feat.shape
    assert T % c_len == 0 and c_len % 8 == 0
    n_double = max(0, (c_len - 1).bit_length() - 1)
    hsum = (jnp.arange(RWKV_W)[:, None] // RWKV_N == jnp.arange(RWKV_W)[None, :] // RWKV_N).astype(BF16)
    tri = (jnp.arange(c_len)[:, None] >= jnp.arange(c_len)[None, :]).astype(BF16)
    row = lambda a: a.reshape(1, -1).astype(F32)
    params = [row(lw['rwkv_mu']), row(lw['rwkv_w0']), lw['rwkv_w2'], row(lw['rwkv_a0']), lw['rwkv_a2'], lw['rwkv_g2'],
              row(lw['rwkv_k_k']), row(lw['rwkv_k_a']), row(lw['rwkv_r_k']), row(lw['rwkv_ln_g']), row(lw['rwkv_ln_b']),
              hsum, tri]
    full = lambda a: pl.BlockSpec(a.shape, lambda b, c: (0,) * a.ndim)
    st_spec = pl.BlockSpec((1, RWKV_HEADS, RWKV_N, RWKV_N), lambda b, c: (b, 0, 0, 0))
    out, s_fin_t = pl.pallas_call(
        functools.partial(_rwkv_kernel, c_len=c_len, n_double=n_double),
        grid=(B, T // c_len),
        in_specs=[pl.BlockSpec((1, c_len, SHIFT_W), lambda b, c: (b, c, 0)),
                  pl.BlockSpec((1, 1, SHIFT_W), lambda b, c: (b, 0, 0)), st_spec] + [full(p) for p in params],
        out_specs=[pl.BlockSpec((1, c_len, RWKV_W), lambda b, c: (b, c, 0)), st_spec],
        out_shape=[jax.ShapeDtypeStruct((B, T, RWKV_W), F32),
                   jax.ShapeDtypeStruct((B, RWKV_HEADS, RWKV_N, RWKV_N), F32)],
        scratch_shapes=[pltpu.VMEM((RWKV_HEADS // RWKV_GROUP, RWKV_GW, RWKV_GW), F32),
                        pltpu.VMEM((8, SHIFT_W), F32)],
        compiler_params=_cparams("parallel", "arbitrary"),
        name="rwkv_mix",
    )(feat, prev[:, None, :], jnp.swapaxes(s0, -1, -2), *params)
    return out, jnp.swapaxes(s_fin_t, -1, -2)


def _rope_inv_tiled():
    half = HEAD_DIM // 2
    inv = ROPE_THETA ** (-jnp.arange(half, dtype=jnp.float32) / half)
    return jnp.tile(inv, LANES // half)[None, :]


def trunk_layer(x, period, offset, lw, paged, shift_prev, wkv0, mk, mv, final_g, is_last):
    B, T, _ = x.shape
    n = B * T
    x2d = x.reshape(n, D_MODEL)
    proj = in_proj(x2d, lw['norm_mix'], lw['w_in_packed'], _rope_inv_tiled(), period=period, offset=offset,
                   attn_operands=paged is None)
    q, k, v, qi, kiwi, feat, ga, gb = proj[:8]
    b3 = lambda a: a.reshape(B, T, a.shape[-1])
    ki = kiwi[:, :IDX_DIM].reshape(B, T, IDX_DIM)
    feat = b3(feat)
    if paged is None:
        k_bf, ki_bf, v_aug = proj[8:]
        o_a = dsa_prompt_attn(b3(q), b3(qi), b3(kiwi), b3(k_bf), b3(v_aug), b3(ki_bf))
    else:
        o_a = dsa_sample_attn(b3(q), b3(qi), b3(kiwi), b3(k), b3(v), *paged)
    o_b, wkv_fin = rwkv_mix(feat, shift_prev, wkv0, lw, c_len=min(T, RWKV_CHUNK))
    shift_last = feat[:, -1]
    x2d = mix_out(x2d, o_a.reshape(n, ATT_W), o_b.reshape(n, RWKV_W), ga, gb, lw['proj_a'], lw['proj_b'], lw['w_out'])
    x = cross_attn_mem(x2d.reshape(B, T, D_MODEL), lw['norm_ca'], lw['w_cq'], mk, mv, lw['w_co'])
    x2d = peer_ffn(x.reshape(n, D_MODEL), lw['norm_ffn'], final_g, lw['peer_wq'], lw['peer_k1'],
                   lw['peer_k2'], lw['peer_u'], lw['peer_vt'], final_norm=is_last)
    new_k = k.reshape(B, T, KV_HEADS, HEAD_DIM)
    new_v = v.reshape(B, T, KV_HEADS, HEAD_DIM)
    return x2d.reshape(B, T, D_MODEL), (new_k, new_v, ki, shift_last, wkv_fin)


def kernel(x_prompt, x_sample, cache_k, cache_v, cache_idx_k, state_shift, state_wkv, cache_mem_k, cache_mem_v, page_table, mem_prompt, norm_mix, w_in, rwkv_mu, rwkv_w0, rwkv_w2, rwkv_a0, rwkv_a2, rwkv_g2, rwkv_k_k, rwkv_k_a, rwkv_r_k, rwkv_ln_g, rwkv_ln_b, proj_a, proj_b, w_out, norm_ca, norm_mem, w_cq, w_mk, w_mv, w_co, norm_ffn, peer_wq, peer_k1, peer_k2, peer_u, peer_v, norm_final):
    B, S, _ = x_prompt.shape
    Bd, T, _ = x_sample.shape
    depth = w_in.shape[0]
    past = page_table.shape[1] * PAGE_SIZE
    xp, xs = x_prompt, x_sample
    st_p_all, st_s_all, mem_all = [], [], []
    bf = lambda a: a.astype(BF16)
    row = lambda a: a.reshape(1, -1)
    final_g = row(norm_final)
    M = mem_prompt.shape[1]
    for l in range(depth):
        lw = {
            'norm_mix': row(norm_mix[l]), 'w_in_packed': _pack_w_in(w_in[l]), 'rwkv_mu': rwkv_mu[l], 'rwkv_w0': rwkv_w0[l],
            'rwkv_w2': rwkv_w2[l], 'rwkv_a0': rwkv_a0[l], 'rwkv_a2': rwkv_a2[l], 'rwkv_g2': rwkv_g2[l],
            'rwkv_k_k': rwkv_k_k[l], 'rwkv_k_a': rwkv_k_a[l], 'rwkv_r_k': rwkv_r_k[l], 'rwkv_ln_g': rwkv_ln_g[l],
            'rwkv_ln_b': rwkv_ln_b[l], 'proj_a': bf(proj_a[l]), 'proj_b': bf(proj_b[l]), 'w_out': bf(w_out[l]),
            'norm_ca': row(norm_ca[l]), 'w_cq': bf(w_cq[l]), 'w_co': bf(w_co[l]), 'norm_ffn': row(norm_ffn[l]),
            'peer_wq': bf(peer_wq[l]), 'peer_k1': bf(peer_k1[l]), 'peer_k2': bf(peer_k2[l]),
            'peer_u': bf(peer_u[l]), 'peer_vt': bf(peer_v[l]),
        }
        is_last = l == depth - 1
        mem_kv = norm_proj(mem_prompt.reshape(B * M, D_MODEL), row(norm_mem[l]),
                           bf(jnp.concatenate([w_mk[l], w_mv[l]], axis=1)))
        mk_p = mem_kv[:, :MEM_W].reshape(B, M, MEM_W)
        mv_p = mem_kv[:, MEM_W:].reshape(B, M, MEM_W)
        xp, st_p = trunk_layer(xp, S, 0, lw, None,
                               jnp.zeros((B, SHIFT_W), F32), jnp.zeros((B, RWKV_HEADS, RWKV_N, RWKV_N), F32),
                               mk_p, mv_p, final_g, is_last)
        paged = (cache_k[l], cache_v[l], cache_idx_k[l], page_table)
        xs, st_s = trunk_layer(xs, T, past, lw, paged, state_shift[l], state_wkv[l],
                               cache_mem_k[l].reshape(Bd, M, MEM_W), cache_mem_v[l].reshape(Bd, M, MEM_W), final_g, is_last)
        st_p_all.append(st_p)
        st_s_all.append(st_s)
        mem_all.append((mk_p.reshape(B, M, MEM_HEADS, MEM_HD), mv_p.reshape(B, M, MEM_HEADS, MEM_HD)))
    y_prompt, y_sample = xp, xs
    stack = lambda lst, i: jnp.stack([s[i] for s in lst])
    return (y_prompt, y_sample,
            stack(st_p_all, 0), stack(st_p_all, 1), stack(st_p_all, 2), stack(st_p_all, 3), stack(st_p_all, 4),
            stack(mem_all, 0), stack(mem_all, 1),
            stack(st_s_all, 0), stack(st_s_all, 1), stack(st_s_all, 2), stack(st_s_all, 3), stack(st_s_all, 4))
```
